```python
import jax, jax.numpy as jnp
from jax import lax
import numpy as np

D_MODEL = 1024
BATCH = 2
SEQ = 16384
DEPTH = 1
DEC_BATCH = 32
DEC_SEQ = 16
PAST_LEN = 1024

CHUNK = 64
H_RET = 8
DK_RET = 64
DV_RET = 128
H_FOX = 8
DH_FOX = 64
Q_BLOCK = 128
ROPE_BASE = 10000.0
PEER_HEADS = 8
PEER_N_KEYS = 128
PEER_N_EXPERTS = PEER_N_KEYS * PEER_N_KEYS
PEER_QUERY_DIM = 256
PEER_HALF = PEER_QUERY_DIM // 2
PEER_TOPK = 16
PEER_TOKEN_BLOCK = 128
FORGET_BIAS_INIT = 2.0
EPS = 1e-6

RET_QK = H_RET * DK_RET
RET_V = H_RET * DV_RET
FOX_W = H_FOX * DH_FOX
IN_SIZES = (RET_QK, RET_QK, RET_V, RET_V, FOX_W, FOX_W, FOX_W, H_FOX, D_MODEL, D_MODEL)
IN_WIDTH = 2 * RET_QK + 2 * RET_V + 3 * FOX_W + H_FOX + 2 * D_MODEL

kernel_name = 'hybrid_retention_fox_peer_stream_step'


def rms_norm(x, g):
    xf = x.astype(jnp.float32)
    y = xf * lax.rsqrt(jnp.mean(xf * xf, axis=-1, keepdims=True) + EPS)
    return (y * g.astype(jnp.float32)).astype(x.dtype)


def rotary(x, pos):
    half = x.shape[-1] // 2
    inv = ROPE_BASE ** (-jnp.arange(half, dtype=jnp.float32) / half)
    ang = pos.astype(jnp.float32)[:, None] * inv[None, :]
    cos = jnp.cos(ang)[None, :, None, :]
    sin = jnp.sin(ang)[None, :, None, :]
    xf = x.astype(jnp.float32)
    x1, x2 = xf[..., :half], xf[..., half:]
    return jnp.concatenate([x1 * cos - x2 * sin, x2 * cos + x1 * sin], axis=-1).astype(x.dtype)


def retention_log_gamma():
    return jnp.log1p(-jnp.exp2(-5.0 - jnp.arange(H_RET, dtype=jnp.float32)))


def project_inputs(xn, pos, p):
    B, T, _ = xn.shape
    h = xn @ p['w_in']
    splits = np.cumsum(IN_SIZES)[:-1].tolist()
    rq, rk, rv, rg, fq, fk, fv, fz, ga, gb = jnp.split(h, splits, axis=-1)
    rq = rotary(rq.reshape(B, T, H_RET, DK_RET), pos)
    rk = rotary(rk.reshape(B, T, H_RET, DK_RET), pos) * (DK_RET ** -0.5)
    rv = rv.reshape(B, T, H_RET, DV_RET)
    fq = rms_norm(fq.reshape(B, T, H_FOX, DH_FOX), p['fox_q_g'])
    fk = rms_norm(fk.reshape(B, T, H_FOX, DH_FOX), p['fox_k_g'])
    fv = fv.reshape(B, T, H_FOX, DH_FOX)
    logf = jax.nn.log_sigmoid((fz + p['b_f']).astype(jnp.float32))
    return rq, rk, rv, rg, fq, fk, fv, logf, ga, gb


def retention_chunk(S, q, k, v, log_gamma):
    L = q.shape[1]
    idx = jnp.arange(L, dtype=jnp.float32)
    diff = idx[:, None] - idx[None, :]
    decay = jnp.where(diff >= 0, jnp.exp(jnp.maximum(diff, 0.0)[None] * log_gamma[:, None, None]), 0.0)
    scores = jnp.einsum('blhd,bmhd->bhlm', q, k) * decay[None]
    inner = jnp.einsum('bhlm,bmhe->blhe', scores, v)
    q_dec = jnp.exp((idx + 1.0)[:, None] * log_gamma[None, :])[None, :, :, None]
    cross = jnp.einsum('blhd,bhde->blhe', q, S) * q_dec
    k_dec = k * jnp.exp((L - 1.0 - idx)[:, None] * log_gamma[None, :])[None, :, :, None]
    S_new = jnp.exp(L * log_gamma)[None, :, None, None] * S + jnp.einsum('blhd,blhe->bhde', k_dec, v)
    return S_new, inner + cross


def retention_prompt(q, k, v):
    B, T, H, dk = q.shape
    dv = v.shape[-1]
    nc = T // CHUNK
    log_gamma = retention_log_gamma()

    def to_chunks(a):
        return a.astype(jnp.float32).reshape(B, nc, CHUNK, *a.shape[2:]).swapaxes(0, 1)

    def step(S, qkv):
        qi, ki, vi = qkv
        return retention_chunk(S, qi, ki, vi, log_gamma)

    S0 = jnp.zeros((B, H, dk, dv), jnp.float32)
    S, o = lax.scan(step, S0, (to_chunks(q), to_chunks(k), to_chunks(v)))
    return o.swapaxes(0, 1).reshape(B, T, H, dv), S


def fox_prompt(q, k, v, logf):
    B, T, H, dh = q.shape
    c = jnp.cumsum(logf, axis=1)
    cT = c.transpose(0, 2, 1)
    nb = T // Q_BLOCK
    qb = q.reshape(B, nb, Q_BLOCK, H, dh).swapaxes(0, 1)
    cb = c.reshape(B, nb, Q_BLOCK, H).swapaxes(0, 1)
    kpos = jnp.arange(T)
    scale = dh ** -0.5

    def block(args):
        i, qi, ci = args
        s = jnp.einsum('bqhd,bkhd->bhqk', qi, k).astype(jnp.float32) * scale
        s = s + ci.transpose(0, 2, 1)[..., None] - cT[:, :, None, :]
        qpos = i * Q_BLOCK + jnp.arange(Q_BLOCK)
        s = jnp.where(kpos[None, :] <= qpos[:, None], s, -jnp.inf)
        pr = jax.nn.softmax(s, axis=-1).astype(v.dtype)
        return jnp.einsum('bhqk,bkhd->bqhd', pr, v)

    o = lax.map(block, (jnp.arange(nb), qb, cb))
    return o.swapaxes(0, 1).reshape(B, T, H, dh)


def fox_sample(q, k_new, v_new, logf_new, k_c, v_c, logf_c):
    P = k_c.shape[1]
    L = q.shape[1]
    dh = q.shape[-1]
    k = jnp.concatenate([k_c, k_new], axis=1)
    v = jnp.concatenate([v_c, v_new], axis=1)
    c = jnp.cumsum(jnp.concatenate([logf_c.astype(jnp.float32), logf_new], axis=1), axis=1)
    cT = c.transpose(0, 2, 1)
    s = jnp.einsum('bqhd,bkhd->bhqk', q, k).astype(jnp.float32) * (dh ** -0.5)
    s = s + cT[:, :, P:, None] - cT[:, :, None, :]
    kpos = jnp.arange(P + L)
    qpos = P + jnp.arange(L)
    s = jnp.where(kpos[None, :] <= qpos[:, None], s, -jnp.inf)
    pr = jax.nn.softmax(s, axis=-1).astype(v.dtype)
    return jnp.einsum('bhqk,bkhd->bqhd', pr, v)


def merge_branches(ret_o, rg, fox_o, ga, gb, p):
    B, T = rg.shape[:2]
    ret_o = rms_norm(ret_o, p['ret_gn_g'])
    ret_y = (jax.nn.silu(rg) * ret_o.reshape(B, T, RET_V)) @ p['w_ret_out']
    fox_y = fox_o.reshape(B, T, FOX_W) @ p['w_fox_out']
    mixed = jax.nn.sigmoid(ga) * ret_y + jax.nn.sigmoid(gb) * fox_y
    return mixed @ p['w_o']


def token_mixer_prompt(xn, p):
    T = xn.shape[1]
    pos = jnp.arange(T)
    rq, rk, rv, rg, fq, fk, fv, logf, ga, gb = project_inputs(xn, pos, p)
    ret_o, S = retention_prompt(rq, rk, rv)
    fox_o = fox_prompt(fq, fk, fv, logf)
    y = merge_branches(ret_o.astype(xn.dtype), rg, fox_o, ga, gb, p)
    return y, S, fk, fv, logf


def token_mixer_sample(xn, S, k_c, v_c, logf_c, p):
    L = xn.shape[1]
    P = k_c.shape[1]
    pos = P + jnp.arange(L)
    rq, rk, rv, rg, fq, fk, fv, logf, ga, gb = project_inputs(xn, pos, p)
    S_new, ret_o = retention_chunk(S.astype(jnp.float32), rq.astype(jnp.float32), rk.astype(jnp.float32),
                                   rv.astype(jnp.float32), retention_log_gamma())
    fox_o = fox_sample(fq, fk, fv, logf, k_c.astype(fk.dtype), v_c.astype(fv.dtype), logf_c)
    y = merge_branches(ret_o.astype(xn.dtype), rg, fox_o, ga, gb, p)
    return y, S_new, fk, fv, logf


def peer_ffn(x, p):
    shp = x.shape
    xt = x.reshape(-1, D_MODEL)
    n = xt.shape[0]
    nb = -(-n // PEER_TOKEN_BLOCK)
    pad = nb * PEER_TOKEN_BLOCK - n
    xt = jnp.pad(xt, ((0, pad), (0, 0))).reshape(nb, PEER_TOKEN_BLOCK, D_MODEL)
    w_q, k1, k2, u_tab, v_tab = p['peer_w_q'], p['peer_sub_k1'], p['peer_sub_k2'], p['peer_u'], p['peer_v']

    def block(xb):
        t = xb.shape[0]
        q = (xb @ w_q).reshape(t, PEER_HEADS, 2, PEER_HALF)
        s1 = jnp.einsum('thc,kc->thk', q[:, :, 0], k1).astype(jnp.float32)
        s2 = jnp.einsum('thc,kc->thk', q[:, :, 1], k2).astype(jnp.float32)
        v1, i1 = lax.top_k(s1, PEER_TOPK)
        v2, i2 = lax.top_k(s2, PEER_TOPK)
        cand = (v1[..., :, None] + v2[..., None, :]).reshape(t, PEER_HEADS, PEER_TOPK * PEER_TOPK)
        cidx = (i1[..., :, None] * PEER_N_KEYS + i2[..., None, :]).reshape(t, PEER_HEADS, PEER_TOPK * PEER_TOPK)
        top_s, sel = lax.top_k(cand, PEER_TOPK)
        eidx = jnp.take_along_axis(cidx, sel, axis=-1)
        g = jax.nn.softmax(top_s, axis=-1)
        act = jax.nn.gelu(jnp.einsum('thkd,td->thk', u_tab[eidx], xb), approximate=False)
        w = (g * act.astype(jnp.float32)).astype(xb.dtype)
        return jnp.einsum('thk,thkd->td', w, v_tab[eidx])

    y = lax.map(block, xt).reshape(-1, D_MODEL)[:n]
    return y.reshape(shp)


def setup_inputs(seed: int = 0) -> dict:
    key = jax.random.key(seed)
    ks = jax.random.split(key, 24)
    f32 = jnp.float32

    def nrm(k, shape, scale):
        return jax.random.normal(k, shape, f32) * scale

    L = DEPTH
    return {
        'x_prompt': nrm(ks[0], (BATCH, SEQ, D_MODEL), 1.0),
        'x_sample': nrm(ks[1], (DEC_BATCH, DEC_SEQ, D_MODEL), 1.0),
        'state_ret': nrm(ks[2], (L, DEC_BATCH, H_RET, DK_RET, DV_RET), 0.5),
        'cache_fox_k': nrm(ks[3], (L, DEC_BATCH, PAST_LEN, H_FOX, DH_FOX), 1.0),
        'cache_fox_v': nrm(ks[4], (L, DEC_BATCH, PAST_LEN, H_FOX, DH_FOX), 1.0),
        'cache_fox_logf': jax.nn.log_sigmoid(FORGET_BIAS_INIT + nrm(ks[5], (L, DEC_BATCH, PAST_LEN, H_FOX), 0.5)),
        'norm1_g': 1.0 + nrm(ks[6], (L, D_MODEL), 0.02),
        'w_in': nrm(ks[7], (L, D_MODEL, IN_WIDTH), D_MODEL ** -0.5),
        'b_f': FORGET_BIAS_INIT + nrm(ks[8], (L, H_FOX), 0.1),
        'fox_q_g': 1.0 + nrm(ks[9], (L, DH_FOX), 0.02),
        'fox_k_g': 1.0 + nrm(ks[10], (L, DH_FOX), 0.02),
        'ret_gn_g': 1.0 + nrm(ks[11], (L, H_RET, DV_RET), 0.02),
        'w_ret_out': nrm(ks[12], (L, RET_V, D_MODEL), RET_V ** -0.5),
        'w_fox_out': nrm(ks[13], (L, FOX_W, D_MODEL), FOX_W ** -0.5),
        'w_o': nrm(ks[14], (L, D_MODEL, D_MODEL), D_MODEL ** -0.5),
        'norm2_g': 1.0 + nrm(ks[15], (L, D_MODEL), 0.02),
        'peer_w_q': nrm(ks[16], (L, D_MODEL, PEER_HEADS * PEER_QUERY_DIM), D_MODEL ** -0.5),
        'peer_sub_k1': nrm(ks[17], (L, PEER_N_KEYS, PEER_HALF), PEER_HALF ** -0.5),
        'peer_sub_k2': nrm(ks[18], (L, PEER_N_KEYS, PEER_HALF), PEER_HALF ** -0.5),
        'peer_u': nrm(ks[19], (L, PEER_N_EXPERTS, D_MODEL), D_MODEL ** -0.5),
        'peer_v': nrm(ks[20], (L, PEER_N_EXPERTS, D_MODEL), D_MODEL ** -0.5),
    }


def reference(x_prompt, x_sample, state_ret, cache_fox_k, cache_fox_v, cache_fox_logf,
              norm1_g, w_in, b_f, fox_q_g, fox_k_g, ret_gn_g, w_ret_out, w_fox_out, w_o,
              norm2_g, peer_w_q, peer_sub_k1, peer_sub_k2, peer_u, peer_v):
    xp = x_prompt
    xs = x_sample
    sp_l, kp_l, vp_l, lp_l = [], [], [], []
    ss_l, ks_l, vs_l, ls_l = [], [], [], []
    for l in range(DEPTH):
        p = {
            'w_in': w_in[l], 'b_f': b_f[l], 'fox_q_g': fox_q_g[l], 'fox_k_g': fox_k_g[l],
            'ret_gn_g': ret_gn_g[l], 'w_ret_out': w_ret_out[l], 'w_fox_out': w_fox_out[l], 'w_o': w_o[l],
            'peer_w_q': peer_w_q[l], 'peer_sub_k1': peer_sub_k1[l], 'peer_sub_k2': peer_sub_k2[l],
            'peer_u': peer_u[l], 'peer_v': peer_v[l],
        }
        mix_p, S_p, k_p, v_p, lf_p = token_mixer_prompt(rms_norm(xp, norm1_g[l]), p)
        xp = xp + mix_p
        xp = xp + peer_ffn(rms_norm(xp, norm2_g[l]), p)
        mix_s, S_s, k_s, v_s, lf_s = token_mixer_sample(rms_norm(xs, norm1_g[l]), state_ret[l], cache_fox_k[l],
                                                        cache_fox_v[l], cache_fox_logf[l], p)
        xs = xs + mix_s
        xs = xs + peer_ffn(rms_norm(xs, norm2_g[l]), p)
        sp_l.append(S_p.astype(state_ret.dtype))
        kp_l.append(k_p.astype(cache_fox_k.dtype))
        vp_l.append(v_p.astype(cache_fox_v.dtype))
        lp_l.append(lf_p.astype(cache_fox_logf.dtype))
        ss_l.append(S_s.astype(state_ret.dtype))
        ks_l.append(k_s.astype(cache_fox_k.dtype))
        vs_l.append(v_s.astype(cache_fox_v.dtype))
        ls_l.append(lf_s.astype(cache_fox_logf.dtype))
    ret_state_prompt = jnp.stack(sp_l, axis=0)
    fox_k_prompt = jnp.stack(kp_l, axis=0)
    fox_v_prompt = jnp.stack(vp_l, axis=0)
    fox_logf_prompt = jnp.stack(lp_l, axis=0)
    ret_state_sample = jnp.stack(ss_l, axis=0)
    fox_k_sample = jnp.stack(ks_l, axis=0)
    fox_v_sample = jnp.stack(vs_l, axis=0)
    fox_logf_sample = jnp.stack(ls_l, axis=0)
    return (xp, xs, ret_state_prompt, fox_k_prompt, fox_v_prompt, fox_logf_prompt,
            ret_state_sample, fox_k_sample, fox_v_sample, fox_logf_sample)
```

```python
import functools
import math

import jax
import jax.numpy as jnp
import numpy as np
from jax import lax
from jax.experimental import pallas as pl
from jax.experimental.pallas import tpu as pltpu

D_MODEL = 1024
H_RET = 8
DK_RET = 64
DV_RET = 128
H_FOX = 8
DH_FOX = 64
ROPE_BASE = 10000.0
PEER_HEADS = 8
PEER_N_KEYS = 128
PEER_HALF = 128
PEER_TOPK = 16
EPS = 1e-6

RET_QK = H_RET * DK_RET
RET_V = H_RET * DV_RET
FOX_W = H_FOX * DH_FOX
N_PAIR = H_RET // 2
LANES = 128
VMEM_LIMIT = 56 * 1024 * 1024

BF16 = jnp.bfloat16
F32 = jnp.float32


def _cparams(sem):
    return pltpu.CompilerParams(dimension_semantics=sem, vmem_limit_bytes=VMEM_LIMIT)


def _const_spec(shape):
    nd = len(shape)
    return pl.BlockSpec(shape, lambda *_: (0,) * nd, pipeline_mode=pl.Buffered(1))


def _rms_rows(x, g):
    ms = jnp.mean(x * x, axis=-1, keepdims=True)
    return x * lax.rsqrt(ms + EPS) * g


def _split3(c):
    hi = c.astype(BF16).astype(F32)
    r = c - hi
    mid = r.astype(BF16).astype(F32)
    lo = (r - mid).astype(BF16).astype(F32)
    return hi, mid, lo


def _proj_ret_kernel(x_ref, g_ref, w_ref, cos_ref, sin_ref,
                     rq_ref, rk_ref, rv_ref, rgs_ref, sga_ref, sgb_ref):
    xn = _rms_rows(x_ref[...], g_ref[...]).astype(BF16)
    cosf = cos_ref[...]
    sinf = sin_ref[...]
    tm = xn.shape[0]
    lane = lax.broadcasted_iota(jnp.int32, (tm, RET_QK), 1)
    first_half = (lane % DK_RET) < (DK_RET // 2)

    def rotary(h):
        sw = jnp.where(first_half, pltpu.roll(h, RET_QK - DK_RET // 2, 1), pltpu.roll(h, DK_RET // 2, 1))
        parts = []
        for gidx in range(RET_QK // LANES):
            sl = slice(gidx * LANES, (gidx + 1) * LANES)
            parts.append(h[:, sl] * cosf + sw[:, sl] * sinf)
        return jnp.concatenate(parts, axis=-1)

    def proj(lo, width):
        return jnp.dot(xn, w_ref[:, lo:lo + width], preferred_element_type=F32)

    rq_ref[...] = rotary(proj(0, RET_QK)).astype(BF16)
    rk_ref[...] = (rotary(proj(RET_QK, RET_QK)) * (DK_RET ** -0.5)).astype(BF16)
    off = 2 * RET_QK
    rv_ref[...] = proj(off, RET_V).astype(BF16)
    rg = proj(off + RET_V, RET_V)
    rgs_ref[...] = (rg * jax.nn.sigmoid(rg)).astype(BF16)
    sga_ref[...] = jax.nn.sigmoid(proj(off + 2 * RET_V, D_MODEL)).astype(BF16)
    sgb_ref[...] = jax.nn.sigmoid(proj(off + 2 * RET_V + D_MODEL, D_MODEL)).astype(BF16)


def _proj_ret(x2d, g1, w_ret, cosf, sinf, tm):
    n = x2d.shape[0]
    nblk = n // tm
    tab_blocks = cosf.shape[0] // tm
    row = lambda w: pl.BlockSpec((tm, w), lambda i: (i, 0))
    tab = pl.BlockSpec((tm, LANES), lambda i: (i % tab_blocks, 0))
    widths = (RET_QK, RET_QK, RET_V, RET_V, D_MODEL, D_MODEL)
    return pl.pallas_call(
        _proj_ret_kernel,
        grid=(nblk,),
        in_specs=[row(D_MODEL), _const_spec((1, D_MODEL)), _const_spec(w_ret.shape), tab, tab],
        out_specs=[row(w) for w in widths],
        out_shape=[jax.ShapeDtypeStruct((n, w), BF16) for w in widths],
        compiler_params=_cparams(("arbitrary",)),
        name="proj_ret",
    )(x2d, g1, w_ret, cosf, sinf)


def _proj_fox_kernel(blocks_per_seq, x_ref, g_ref, w_ref, bf_ref, gq_ref, gkp_ref, gkd_ref, pmean_ref, tri_ref,
                     fk_ref, fv_ref, logf_ref, qp_ref, kp_ref, vp_ref, carry_ref):
    i = pl.program_id(0)
    xn = _rms_rows(x_ref[...], g_ref[...]).astype(BF16)
    tm = xn.shape[0]

    def proj(lo, width):
        return jnp.dot(xn, w_ref[:, lo:lo + width], preferred_element_type=F32)

    hk = proj(0, FOX_W)
    sq = hk * hk
    sq_hi = sq.astype(BF16)
    sq_lo = (sq - sq_hi.astype(F32)).astype(BF16)
    ms = (jnp.dot(sq_hi, pmean_ref[...], preferred_element_type=F32)
          + jnp.dot(sq_lo, pmean_ref[...], preferred_element_type=F32))
    fk_ref[...] = hk * lax.rsqrt(ms + EPS) * gkd_ref[...]
    fv_ref[...] = proj(FOX_W, FOX_W)

    z = proj(2 * FOX_W, LANES) + bf_ref[...]
    logf = jnp.minimum(z, 0.0) - jnp.log1p(jnp.exp(-jnp.abs(z)))
    logf_ref[...] = logf

    @pl.when(i % blocks_per_seq == 0)
    def _():
        carry_ref[...] = jnp.zeros_like(carry_ref)

    l_hi, l_mid, l_lo = _split3(logf)
    tri = tri_ref[...]
    c = (jnp.dot(tri, l_hi.astype(BF16), preferred_element_type=F32)
         + jnp.dot(tri, l_mid.astype(BF16), preferred_element_type=F32)
         + jnp.dot(tri, l_lo.astype(BF16), preferred_element_type=F32)) + carry_ref[...]
    carry_ref[...] = c[tm - 1:tm, :]

    lane = lax.broadcasted_iota(jnp.int32, (tm, LANES), 1)
    is_vec = lane < DH_FOX
    base = 2 * FOX_W + LANES
    for h in range(H_FOX):
        ch = jnp.broadcast_to(c[:, h:h + 1], (tm, LANES))
        c_hi, c_mid, c_lo = _split3(ch)
        one = jnp.ones_like(ch)
        zero = jnp.zeros_like(ch)
        aug_q = jnp.where(lane == 64, c_hi, jnp.where(lane == 65, c_mid, jnp.where(
            lane == 66, c_lo, jnp.where(lane < 70, one, zero))))
        aug_k = jnp.where(lane < 67, one, jnp.where(lane == 67, -c_hi, jnp.where(
            lane == 68, -c_mid, jnp.where(lane == 69, -c_lo, zero))))
        hq = proj(base + h * LANES, LANES)
        hq = hq * lax.rsqrt(jnp.sum(hq * hq, axis=-1, keepdims=True) * (1.0 / DH_FOX) + EPS) * gq_ref[...]
        qp_ref[h] = jnp.where(is_vec, hq * (DH_FOX ** -0.5), aug_q).astype(BF16)
        hkp = proj(base + (H_FOX + h) * LANES, LANES)
        hkp = hkp * lax.rsqrt(jnp.sum(hkp * hkp, axis=-1, keepdims=True) * (1.0 / DH_FOX) + EPS) * gkp_ref[...]
        kp_ref[h] = jnp.where(is_vec, hkp, aug_k).astype(BF16)
        hv = proj(base + (2 * H_FOX + h) * LANES, LANES)
        vp_ref[h] = jnp.where(is_vec, hv, one).astype(BF16)


def _proj_fox(x2d, g1, w_fox, bf_pad, gq_pad, gk_pad, gk_dense, pmean, tri, tm, blocks_per_seq):
    n = x2d.shape[0]
    nblk = n // tm
    row = lambda w: pl.BlockSpec((tm, w), lambda i: (i, 0))
    hm = pl.BlockSpec((H_FOX, tm, LANES), lambda i: (0, i, 0))
    return pl.pallas_call(
        functools.partial(_proj_fox_kernel, blocks_per_seq),
        grid=(nblk,),
        in_specs=[row(D_MODEL), _const_spec((1, D_MODEL)), _const_spec(w_fox.shape), _const_spec((1, LANES)),
                  _const_spec((1, LANES)), _const_spec((1, LANES)), _const_spec((1, FOX_W)),
                  _const_spec((FOX_W, FOX_W)), _const_spec((tm, tm))],
        out_specs=[row(FOX_W), row(FOX_W), row(LANES), hm, hm, hm],
        out_shape=[jax.ShapeDtypeStruct((n, FOX_W), F32), jax.ShapeDtypeStruct((n, FOX_W), F32),
                   jax.ShapeDtypeStruct((n, LANES), F32)]
        + [jax.ShapeDtypeStruct((H_FOX, n, LANES), BF16)] * 3,
        scratch_shapes=[pltpu.VMEM((1, LANES), F32)],
        compiler_params=_cparams(("arbitrary",)),
        name="proj_fox",
    )(x2d, g1, w_fox, bf_pad, gq_pad, gk_pad, gk_dense, pmean, tri)


def _rotary_tables(pos):
    half = DK_RET // 2
    inv = ROPE_BASE ** (-jnp.arange(half, dtype=F32) / half)
    ang = pos.astype(F32)[:, None] * inv[None, :]
    cos = jnp.cos(ang)
    sin = jnp.sin(ang)
    cosf = jnp.concatenate([cos, cos, cos, cos], axis=-1)
    sinf = jnp.concatenate([-sin, sin, -sin, sin], axis=-1)
    return cosf, sinf


def _pad_heads(w, n_heads, dh):
    d = w.shape[0]
    w3 = w.reshape(d, n_heads, dh)
    return jnp.pad(w3, ((0, 0), (0, 0), (0, LANES - dh))).reshape(d, n_heads * LANES)


def _prep_mixer_params(w_in, b_f, fox_q_g, fox_k_g):
    sizes = (RET_QK, RET_QK, RET_V, RET_V, FOX_W, FOX_W, FOX_W, H_FOX, D_MODEL, D_MODEL)
    offs = np.concatenate([[0], np.cumsum(sizes)])
    col = lambda j: w_in[:, offs[j]:offs[j + 1]]
    w_ret = jnp.concatenate([col(0), col(1), col(2), col(3), col(8), col(9)], axis=1).astype(BF16)
    w_fz = jnp.pad(col(7), ((0, 0), (0, LANES - H_FOX)))
    w_fox = jnp.concatenate([col(5), col(6), w_fz, _pad_heads(col(4), H_FOX, DH_FOX),
                             _pad_heads(col(5), H_FOX, DH_FOX), _pad_heads(col(6), H_FOX, DH_FOX)],
                            axis=1).astype(BF16)
    bf_pad = jnp.pad(b_f, (0, LANES - H_FOX)).reshape(1, LANES)
    gq_pad = jnp.pad(fox_q_g, (0, LANES - DH_FOX)).reshape(1, LANES)
    gk_pad = jnp.pad(fox_k_g, (0, LANES - DH_FOX)).reshape(1, LANES)
    gk_dense = jnp.tile(fox_k_g, H_FOX).reshape(1, FOX_W)
    grp = np.arange(FOX_W) // DH_FOX
    pmean = jnp.asarray((grp[:, None] == grp[None, :]).astype(np.float32) / DH_FOX, dtype=BF16)
    return w_ret, w_fox, bf_pad, gq_pad, gk_pad, gk_dense, pmean


def _tri(tm):
    r = np.arange(tm)
    return jnp.asarray((r[None, :] <= r[:, None]).astype(np.float32), dtype=BF16)


def _retention_tables(chunk):
    log_gamma = jnp.log1p(-jnp.exp2(-5.0 - jnp.arange(H_RET, dtype=F32)))
    idx = jnp.arange(chunk, dtype=F32)
    diff = idx[:, None] - idx[None, :]
    dmat = jnp.where(diff >= 0, jnp.exp(jnp.maximum(diff, 0.0)[None] * log_gamma[:, None, None]), 0.0)
    lg_qk = jnp.repeat(log_gamma, DK_RET).reshape(N_PAIR, 1, 2 * DK_RET)
    qdec = jnp.exp((idx + 1.0)[None, :, None] * lg_qk)
    kdec = jnp.exp((chunk - 1.0 - idx)[None, :, None] * lg_qk)
    sdec = jnp.broadcast_to(jnp.exp(chunk * lg_qk).reshape(N_PAIR, 2 * DK_RET, 1), (N_PAIR, 2 * DK_RET, 2 * DV_RET))
    row_head = np.arange(2 * DK_RET) // DK_RET
    col_head = np.arange(2 * DV_RET) // DV_RET
    bmask = jnp.asarray((row_head[:, None] == col_head[None, :]).astype(np.float32))
    return dmat, qdec, kdec, sdec, bmask


def _retention_kernel(q_ref, k_ref, v_ref, s0_ref, dmat_ref, qdec_ref, kdec_ref, sdec_ref, bmask_ref,
                      o_ref, sout_ref, s_scr):
    c = pl.program_id(2)

    @pl.when(c == 0)
    def _():
        s_scr[...] = s0_ref[0, 0]

    q = q_ref[...]
    k = k_ref[...]
    v = v_ref[...]
    lane = lax.broadcasted_iota(jnp.int32, q.shape, 1)
    zero = jnp.zeros_like(q)
    nt = (((1,), (1,)), ((), ()))
    halves = []
    for j in range(2):
        qj = jnp.where((lane < DK_RET) == (j == 0), q, zero)
        sc = lax.dot_general(qj, k, nt, preferred_element_type=F32) * dmat_ref[j]
        halves.append(jnp.dot(sc.astype(BF16), v[:, j * DV_RET:(j + 1) * DV_RET], preferred_element_type=F32))
    inner = jnp.concatenate(halves, axis=-1)
    s = s_scr[...]
    qd = (q.astype(F32) * qdec_ref[0]).astype(BF16)
    cross = jnp.dot(qd, s.astype(BF16), preferred_element_type=F32)
    o_ref[...] = (inner + cross).astype(o_ref.dtype)
    kd = (k.astype(F32) * kdec_ref[0]).astype(BF16)
    upd = lax.dot_general(kd, v, (((0,), (0,)), ((), ())), preferred_element_type=F32)
    s_new = sdec_ref[0] * s + bmask_ref[...] * upd
    s_scr[...] = s_new

    @pl.when(c == pl.num_programs(2) - 1)
    def _():
        sout_ref[0, 0] = s_new


def _retention(rq, rk, rv, s0_pair, batch, seq, chunk):
    n = rq.shape[0]
    nchunk = seq // chunk
    dmat, qdec, kdec, sdec, bmask = _retention_tables(chunk)
    qk_spec = pl.BlockSpec((chunk, 2 * DK_RET), lambda b, p, c: (b * nchunk + c, p))
    v_spec = pl.BlockSpec((chunk, 2 * DV_RET), lambda b, p, c: (b * nchunk + c, p))
    s_spec = pl.BlockSpec((1, 1, 2 * DK_RET, 2 * DV_RET), lambda b, p, c: (b, p, 0, 0))
    pair_tab = lambda shape: pl.BlockSpec((1,) + shape, lambda b, p, c: (p, 0, 0))
    return pl.pallas_call(
        _retention_kernel,
        grid=(batch, N_PAIR, nchunk),
        in_specs=[qk_spec, qk_spec, v_spec, s_spec,
                  pl.BlockSpec((2, chunk, chunk), lambda b, p, c: (p, 0, 0)),
                  pair_tab((chunk, 2 * DK_RET)), pair_tab((chunk, 2 * DK_RET)),
                  pair_tab((2 * DK_RET, 2 * DV_RET)),
                  pl.BlockSpec((2 * DK_RET, 2 * DV_RET), lambda b, p, c: (0, 0))],
        out_specs=[v_spec, s_spec],
        out_shape=[jax.ShapeDtypeStruct((n, RET_V), BF16),
                   jax.ShapeDtypeStruct((batch, N_PAIR, 2 * DK_RET, 2 * DV_RET), F32)],
        scratch_shapes=[pltpu.VMEM((2 * DK_RET, 2 * DV_RET), F32)],
        compiler_params=_cparams(("arbitrary", "arbitrary", "arbitrary")),
        name="retention",
    )(rq, rk, rv, s0_pair, dmat, qdec, kdec, sdec, bmask)


NEG_BIG = -1e30


def _fox_attn_kernel(qi_ref, ki_ref, q_ref, k_ref, v_ref, o_ref, m_scr, acc_scr):
    s_idx = pl.program_id(2)
    qi = qi_ref[s_idx]
    ki = ki_ref[s_idx]

    @pl.when(ki == 0)
    def _():
        m_scr[...] = jnp.full_like(m_scr, NEG_BIG)
        acc_scr[...] = jnp.zeros_like(acc_scr)

    q = q_ref[0]
    k = k_ref[0]
    s = lax.dot_general(q, k, (((1,), (1,)), ((), ())), preferred_element_type=F32)

    def update(s):
        m_prev = m_scr[...]
        m_new = jnp.maximum(m_prev, jnp.max(s, axis=-1, keepdims=True))
        alpha = jnp.exp(m_prev - m_new)
        p = jnp.exp(s - m_new)
        acc_scr[...] = alpha * acc_scr[...] + jnp.dot(p.astype(BF16), v_ref[0], preferred_element_type=F32)
        m_scr[...] = m_new

    @pl.when(ki < qi)
    def _():
        update(s)

    @pl.when(ki == qi)
    def _():
        row = lax.broadcasted_iota(jnp.int32, s.shape, 0)
        col = lax.broadcasted_iota(jnp.int32, s.shape, 1)
        update(jnp.where(col <= row, s, -jnp.inf))
        acc = acc_scr[...]
        o_ref[0] = (acc / acc[:, DH_FOX:DH_FOX + 1]).astype(o_ref.dtype)


def _fox_attention(qp, kp, vp, batch, seq, tq):
    n = qp.shape[1]
    nq = seq // tq
    qi_tab = np.concatenate([np.full(i + 1, i, np.int32) for i in range(nq)])
    ki_tab = np.concatenate([np.arange(i + 1, dtype=np.int32) for i in range(nq)])
    q_spec = pl.BlockSpec((1, tq, LANES), lambda h, b, s, qi, ki: (h, b * nq + qi[s], 0))
    kv_spec = pl.BlockSpec((1, tq, LANES), lambda h, b, s, qi, ki: (h, b * nq + ki[s], 0))
    return pl.pallas_call(
        _fox_attn_kernel,
        grid_spec=pltpu.PrefetchScalarGridSpec(
            num_scalar_prefetch=2,
            grid=(H_FOX, batch, len(qi_tab)),
            in_specs=[q_spec, kv_spec, kv_spec],
            out_specs=q_spec,
            scratch_shapes=[pltpu.VMEM((tq, 1), F32), pltpu.VMEM((tq, LANES), F32)],
        ),
        out_shape=jax.ShapeDtypeStruct((H_FOX, n, LANES), BF16),
        compiler_params=_cparams(("arbitrary", "arbitrary", "arbitrary")),
        name="fox_attention",
    )(jnp.asarray(qi_tab), jnp.asarray(ki_tab), qp, kp, vp)


def _fox_sample_kernel(qp_ref, kp_ref, vp_ref, kc_ref, vc_ref, lfc_ref, lfn_ref, tric_ref, trin_ref, place_ref,
                       o_ref):
    n_new = qp_ref.shape[1]
    past = kc_ref.shape[1]
    rows = H_FOX * n_new
    q_rows = []
    k_new = jnp.zeros((n_new, FOX_W), F32)
    v_new = jnp.zeros((n_new, FOX_W), F32)
    for h in range(H_FOX):
        q_rows.append(jnp.dot(qp_ref[h], place_ref[h], preferred_element_type=F32))
        k_new = k_new + jnp.dot(kp_ref[h], place_ref[h], preferred_element_type=F32)
        v_new = v_new + jnp.dot(vp_ref[h], place_ref[h], preferred_element_type=F32)
    q_wide = jnp.concatenate(q_rows, axis=0).astype(BF16)
    k_new = k_new.astype(BF16)
    v_new = v_new.astype(BF16)

    nt = (((1,), (1,)), ((), ()))
    s_c = lax.dot_general(q_wide, kc_ref[0].astype(BF16), nt, preferred_element_type=F32)
    s_n = lax.dot_general(q_wide, k_new, nt, preferred_element_type=F32)

    def cumsum_lanes(x, tri):
        hi, mid, lo = _split3(x)
        return (jnp.dot(hi.astype(BF16), tri, preferred_element_type=F32)
                + jnp.dot(mid.astype(BF16), tri, preferred_element_type=F32)
                + jnp.dot(lo.astype(BF16), tri, preferred_element_type=F32))

    c_c = cumsum_lanes(lfc_ref[0], tric_ref[...])
    c_n = cumsum_lanes(lfn_ref[0], trin_ref[...]) + c_c[:, past - 1:past]

    def per_row(x):
        return jnp.concatenate([jnp.broadcast_to(x[h:h + 1], (n_new, x.shape[1])) for h in range(H_FOX)], axis=0)

    cn_rows = per_row(c_n)
    colq = lax.broadcasted_iota(jnp.int32, (rows, n_new), 1)
    rowq = lax.broadcasted_iota(jnp.int32, (rows, n_new), 0) % n_new
    c_q = jnp.sum(jnp.where(colq == rowq, cn_rows, 0.0), axis=-1, keepdims=True)

    s_c = s_c + c_q - per_row(c_c)
    s_n = jnp.where(colq <= rowq, s_n + c_q - cn_rows, -jnp.inf)
    m = jnp.maximum(jnp.max(s_c, axis=-1, keepdims=True), jnp.max(s_n, axis=-1, keepdims=True))
    p_c = jnp.exp(s_c - m)
    p_n = jnp.exp(s_n - m)
    denom = jnp.sum(p_c, axis=-1, keepdims=True) + jnp.sum(p_n, axis=-1, keepdims=True)
    pv = (jnp.dot(p_c.astype(BF16), vc_ref[0].astype(BF16), preferred_element_type=F32)
          + jnp.dot(p_n.astype(BF16), v_new, preferred_element_type=F32)) / denom
    pv = pv.astype(BF16)
    for h in range(H_FOX):
        o_ref[h] = lax.dot_general(pv[h * n_new:(h + 1) * n_new], place_ref[h], nt,
                                   preferred_element_type=F32).astype(o_ref.dtype)


def _fox_sample(qp, kp, vp, k_cache, v_cache, lf_cache_t, lf_new_t, n_streams, n_new):
    past = k_cache.shape[1]
    r = np.arange(past)
    tric = jnp.asarray((r[:, None] <= r[None, :]).astype(np.float32), dtype=BF16)
    rn = np.arange(n_new)
    trin = jnp.asarray((rn[:, None] <= rn[None, :]).astype(np.float32), dtype=BF16)
    place_np = np.zeros((H_FOX, LANES, FOX_W), np.float32)
    for h in range(H_FOX):
        place_np[h, np.arange(DH_FOX), h * DH_FOX + np.arange(DH_FOX)] = 1.0
    place = jnp.asarray(place_np, dtype=BF16)
    hm = pl.BlockSpec((H_FOX, n_new, LANES), lambda s: (0, s, 0))
    cache = pl.BlockSpec((1, past, FOX_W), lambda s: (s, 0, 0))
    return pl.pallas_call(
        _fox_sample_kernel,
        grid=(n_streams,),
        in_specs=[hm, hm, hm, cache, cache,
                  pl.BlockSpec((1, H_FOX, past), lambda s: (s, 0, 0)),
                  pl.BlockSpec((1, H_FOX, n_new), lambda s: (s, 0, 0)),
                  _const_spec((past, past)), _const_spec((n_new, n_new)), _const_spec((H_FOX, LANES, FOX_W))],
        out_specs=hm,
        out_shape=jax.ShapeDtypeStruct((H_FOX, n_streams * n_new, LANES), BF16),
        compiler_params=_cparams(("arbitrary",)),
        name="fox_sample",
    )(qp, kp, vp, k_cache, v_cache, lf_cache_t, lf_new_t, tric, trin, place)


def _merge_kernel(x_ref, ro_ref, rgs_ref, fo_ref, sga_ref, sgb_ref, gn_ref, wro_ref, wfo_ref, wo_ref, g2_ref,
                  x1_ref, xn2_ref):
    ro = ro_ref[...].astype(F32)
    parts = []
    for h in range(H_RET):
        sl = slice(h * DV_RET, (h + 1) * DV_RET)
        parts.append(_rms_rows(ro[:, sl], gn_ref[:, sl]))
    gated = (rgs_ref[...].astype(F32) * jnp.concatenate(parts, axis=-1)).astype(BF16)
    ret_y = jnp.dot(gated, wro_ref[...], preferred_element_type=F32)
    fox_y = jnp.dot(fo_ref[0], wfo_ref[0], preferred_element_type=F32)
    for h in range(1, H_FOX):
        fox_y = fox_y + jnp.dot(fo_ref[h], wfo_ref[h], preferred_element_type=F32)
    mixed = sga_ref[...].astype(F32) * ret_y + sgb_ref[...].astype(F32) * fox_y
    x1 = x_ref[...] + jnp.dot(mixed.astype(BF16), wo_ref[...], preferred_element_type=F32)
    x1_ref[...] = x1
    xn2_ref[...] = _rms_rows(x1, g2_ref[...]).astype(BF16)


def _merge(x2d, ret_o, rgs, fox_o, sga, sgb, gn, w_ro, w_fo_pad, w_o, g2, tm):
    n = x2d.shape[0]
    row = lambda w: pl.BlockSpec((tm, w), lambda i: (i, 0))
    return pl.pallas_call(
        _merge_kernel,
        grid=(n // tm,),
        in_specs=[row(D_MODEL), row(RET_V), row(RET_V), pl.BlockSpec((H_FOX, tm, LANES), lambda i: (0, i, 0)),
                  row(D_MODEL), row(D_MODEL), _const_spec((1, RET_V)), _const_spec(w_ro.shape),
                  _const_spec(w_fo_pad.shape), _const_spec(w_o.shape), _const_spec((1, D_MODEL))],
        out_specs=[row(D_MODEL), row(D_MODEL)],
        out_shape=[jax.ShapeDtypeStruct((n, D_MODEL), F32), jax.ShapeDtypeStruct((n, D_MODEL), BF16)],
        compiler_params=_cparams(("arbitrary",)),
        name="merge",
    )(x2d, ret_o, rgs, fox_o, sga, sgb, gn, w_ro, w_fo_pad, w_o, g2)


_CAND_ROWS = ([(0, j) for j in range(16)] + [(i, j) for i in range(1, 8) for j in range(8)]
              + [(i, 0) for i in range(8, 16)])
_CAND_VALID = np.array([(i + 1) * (j + 1) <= PEER_TOPK for i, j in _CAND_ROWS])
_CAND_FLAT = np.array([i * PEER_TOPK + j for i, j in _CAND_ROWS], np.float32)


def _extract_topk(scores, order, k):
    cur = scores
    rank = jnp.full(scores.shape, float(k), F32)
    vals = []
    big = float(2 ** 20)
    for r in range(k):
        m = jnp.max(cur, axis=0, keepdims=True)
        first = jnp.min(jnp.where(cur == m, order, big), axis=0, keepdims=True)
        sel = order == first
        rank = jnp.where(sel, float(r), rank)
        cur = jnp.where(sel, -jnp.inf, cur)
        vals.append(m)
    return vals, rank


def _peer_route_kernel(xn_ref, wq_ref, k1_ref, k2_ref, cflat_ref, cvalid_ref,
                       n1_ref, e1_ref, r2_ref, e2_ref, q_scr):
    tm = xn_ref.shape[0]
    q_scr[...] = jnp.dot(xn_ref[...], wq_ref[...], preferred_element_type=F32).astype(BF16)
    key_iota = lax.broadcasted_iota(jnp.int32, (PEER_N_KEYS, tm), 0).astype(F32)
    cflat = jnp.broadcast_to(cflat_ref[...], (len(_CAND_ROWS), tm))
    cvalid = jnp.broadcast_to(cvalid_ref[...], (len(_CAND_ROWS), tm)) > 0.5
    nt = (((1,), (1,)), ((), ()))

    def head(h, carry):
        col = pl.multiple_of(h * (2 * PEER_HALF), 2 * PEER_HALF)
        s1 = lax.dot_general(k1_ref[...], q_scr[:, pl.ds(col, PEER_HALF)], nt, preferred_element_type=F32)
        s2 = lax.dot_general(k2_ref[...], q_scr[:, pl.ds(col + PEER_HALF, PEER_HALF)], nt,
                             preferred_element_type=F32)
        v1, rank1 = _extract_topk(s1, key_iota, PEER_TOPK)
        v2, rank2 = _extract_topk(s2, key_iota, PEER_TOPK)
        v2_lo = jnp.concatenate(v2[:8], axis=0)
        v2_all = jnp.concatenate(v2, axis=0)
        cand = jnp.concatenate([v1[0] + v2_all] + [v1[i] + v2_lo for i in range(1, 8)]
                               + [jnp.concatenate(v1[8:], axis=0) + v2[0]], axis=0)
        cand = jnp.where(cvalid, cand, -jnp.inf)
        _, crank = _extract_topk(cand, cflat, PEER_TOPK)
        picked = crank < float(PEER_TOPK)
        top = v1[0] + v2[0]
        z = jnp.sum(jnp.where(picked, jnp.exp(cand - top), 0.0), axis=0, keepdims=True)
        cnt = jnp.where(picked, 1.0, 0.0)
        n_rows = [jnp.sum(cnt[0:16], axis=0, keepdims=True)]
        n_rows += [jnp.sum(cnt[16 + 8 * (i - 1):16 + 8 * i], axis=0, keepdims=True) for i in range(1, 8)]
        n_rows += [cnt[72 + i:73 + i] for i in range(8)]
        n1 = jnp.zeros_like(s1)
        for i in range(PEER_TOPK):
            n1 = jnp.where(rank1 == float(i), n_rows[i], n1)
        n1_ref[h] = n1
        e1_ref[h] = jnp.exp(s1 - v1[0]) / z
        r2_ref[h] = rank2.astype(BF16)
        e2_ref[h] = jnp.exp(s2 - v2[0]).astype(BF16)
        return carry

    lax.fori_loop(0, PEER_HEADS, head, 0)


def _peer_route(xn2, w_q, k1, k2, tm):
    n = xn2.shape[0]
    ncand = len(_CAND_ROWS)
    cflat = jnp.asarray(_CAND_FLAT).reshape(ncand, 1)
    cvalid = jnp.asarray(_CAND_VALID.astype(np.float32)).reshape(ncand, 1)
    em = pl.BlockSpec((PEER_HEADS, PEER_N_KEYS, tm), lambda i: (0, 0, i))
    return pl.pallas_call(
        _peer_route_kernel,
        grid=(n // tm,),
        in_specs=[pl.BlockSpec((tm, D_MODEL), lambda i: (i, 0)), _const_spec(w_q.shape), _const_spec(k1.shape),
                  _const_spec(k2.shape), _const_spec((ncand, 1)), _const_spec((ncand, 1))],
        out_specs=[em, em, em, em],
        out_shape=[jax.ShapeDtypeStruct((PEER_HEADS, PEER_N_KEYS, n), dt) for dt in (F32, F32, BF16, BF16)],
        scratch_shapes=[pltpu.VMEM((tm, PEER_HEADS * 2 * PEER_HALF), BF16)],
        compiler_params=_cparams(("arbitrary",)),
        name="peer_route",
    )(xn2, w_q, k1, k2, cflat, cvalid)


def _peer_dense_kernel(xn_ref, x1_ref, u_ref, vt_ref, n1_ref, e1_ref, r2_ref, e2_ref, y_ref, acc_scr):
    e = pl.program_id(1)
    te = u_ref.shape[0]
    tm = xn_ref.shape[0]

    @pl.when(e == 0)
    def _():
        acc_scr[...] = jnp.zeros_like(acc_scr)

    zt = lax.dot_general(u_ref[...], xn_ref[...], (((1,), (1,)), ((), ())), preferred_element_type=F32)
    act = (0.5 * zt * (1.0 + lax.erf(zt * (2.0 ** -0.5)))).astype(BF16)
    zero = jnp.zeros((PEER_N_KEYS, tm), BF16)
    gates = []
    for a_loc in range(te // PEER_N_KEYS):
        a = e * (te // PEER_N_KEYS) + a_loc
        g = zero
        for h in range(PEER_HEADS):
            n1 = n1_ref[h, pl.ds(a, 1), :].astype(BF16)
            e1 = e1_ref[h, pl.ds(a, 1), :].astype(BF16)
            g = g + jnp.where(r2_ref[h] < n1, e2_ref[h], zero) * e1
        gates.append(g)
    w = act * jnp.concatenate(gates, axis=0)
    acc_scr[...] += jnp.dot(vt_ref[...], w, preferred_element_type=F32)

    @pl.when(e == pl.num_programs(1) - 1)
    def _():
        y_ref[...] = x1_ref[...] + acc_scr[...].T


def _peer_dense(xn2, x1, u_bf, vt_bf, n1, e1, r2, e2, tm, te):
    n = xn2.shape[0]
    n_exp = u_bf.shape[0]
    em = pl.BlockSpec((PEER_HEADS, PEER_N_KEYS, tm), lambda i, e: (0, 0, i))
    return pl.pallas_call(
        _peer_dense_kernel,
        grid=(n // tm, n_exp // te),
        in_specs=[pl.BlockSpec((tm, D_MODEL), lambda i, e: (i, 0)), pl.BlockSpec((tm, D_MODEL), lambda i, e: (i, 0)),
                  pl.BlockSpec((te, D_MODEL), lambda i, e: (e, 0)), pl.BlockSpec((D_MODEL, te), lambda i, e: (0, e)),
                  em, em, em, em],
        out_specs=pl.BlockSpec((tm, D_MODEL), lambda i, e: (i, 0)),
        out_shape=jax.ShapeDtypeStruct((n, D_MODEL), F32),
        scratch_shapes=[pltpu.VMEM((D_MODEL, tm), F32)],
        compiler_params=_cparams(("arbitrary", "arbitrary")),
        name="peer_dense",
    )(xn2, x1, u_bf, vt_bf, n1, e1, r2, e2)


def _state_to_pair(s):
    b = s.shape[0]
    s5 = s.reshape(b, N_PAIR, 2, DK_RET, DV_RET)
    z = jnp.zeros_like(s5[:, :, 0])
    top = jnp.concatenate([s5[:, :, 0], z], axis=-1)
    bot = jnp.concatenate([z, s5[:, :, 1]], axis=-1)
    return jnp.concatenate([top, bot], axis=-2)


def _pair_to_state(sp):
    b = sp.shape[0]
    a = sp[:, :, :DK_RET, :DV_RET]
    c = sp[:, :, DK_RET:, DV_RET:]
    return jnp.stack([a, c], axis=2).reshape(b, H_RET, DK_RET, DV_RET)


ROW_TILE = 512
RET_CHUNK = 256
ATTN_TILE = 512
PEER_ROUTE_TILE = 256
PEER_TOKEN_TILE = 512
PEER_EXPERT_TILE = 512


def _layer(x, pos, seq, s0_pair, fox_cache, prm):
    batch = x.shape[0]
    n = batch * seq
    x2d = x.reshape(n, D_MODEL)
    tm = min(ROW_TILE, n)
    cosf, sinf = _rotary_tables(pos)
    if fox_cache is not None:
        cosf = jnp.tile(cosf, (batch, 1))
        sinf = jnp.tile(sinf, (batch, 1))
    rq, rk, rv, rgs, sga, sgb = _proj_ret(x2d, prm['g1'], prm['w_ret'], cosf, sinf, tm)
    blocks_per_seq = max(seq // tm, 1)
    fk, fv, logf_pad, qp, kp, vp = _proj_fox(x2d, prm['g1'], prm['w_fox'], prm['bf_pad'], prm['gq_pad'],
                                             prm['gk_pad'], prm['gk_dense'], prm['pmean'], _tri(tm), tm,
                                             blocks_per_seq)
    logf = logf_pad[:, :H_FOX]
    if fox_cache is None:
        ret_o, s_pair = _retention(rq, rk, rv, s0_pair, batch, seq, min(RET_CHUNK, seq))
        fox_o = _fox_attention(qp, kp, vp, batch, seq, min(ATTN_TILE, seq))
    else:
        ret_o, s_pair = _retention(rq, rk, rv, s0_pair, batch, seq, seq)
        k_c, v_c, lf_c = fox_cache
        past = k_c.shape[1]
        fox_o = _fox_sample(qp, kp, vp, k_c.reshape(batch, past, FOX_W), v_c.reshape(batch, past, FOX_W),
                            lf_c.transpose(0, 2, 1), logf.reshape(batch, seq, H_FOX).transpose(0, 2, 1), batch, seq)
    x1, xn2 = _merge(x2d, ret_o, rgs, fox_o, sga, sgb, prm['gn'], prm['w_ro'], prm['w_fo_pad'], prm['w_o'],
                     prm['g2'], tm)
    n1, e1, r2, e2 = _peer_route(xn2, prm['w_q'], prm['k1'], prm['k2'], min(PEER_ROUTE_TILE, n))
    y = _peer_dense(xn2, x1, prm['u'], prm['vt'], n1, e1, r2, e2, min(PEER_TOKEN_TILE, n), PEER_EXPERT_TILE)
    return (y.reshape(batch, seq, D_MODEL), _pair_to_state(s_pair), fk.reshape(batch, seq, H_FOX, DH_FOX),
            fv.reshape(batch, seq, H_FOX, DH_FOX), logf.reshape(batch, seq, H_FOX))


def kernel(x_prompt, x_sample, state_ret, cache_fox_k, cache_fox_v, cache_fox_logf, norm1_g, w_in, b_f, fox_q_g,
           fox_k_g, ret_gn_g, w_ret_out, w_fox_out, w_o, norm2_g, peer_w_q, peer_sub_k1, peer_sub_k2, peer_u,
           peer_v):
    depth = w_in.shape[0]
    batch, seq, _ = x_prompt.shape
    dec_batch, dec_seq, _ = x_sample.shape
    past = cache_fox_k.shape[2]
    xp, xs = x_prompt, x_sample
    outs = [[] for _ in range(8)]
    for l in range(depth):
        w_ret, w_fox, bf_pad, gq_pad, gk_pad, gk_dense, pmean = _prep_mixer_params(w_in[l], b_f[l], fox_q_g[l],
                                                                                  fox_k_g[l])
        prm = dict(
            g1=norm1_g[l].reshape(1, D_MODEL), w_ret=w_ret, w_fox=w_fox, bf_pad=bf_pad, gq_pad=gq_pad, gk_pad=gk_pad,
            gk_dense=gk_dense, pmean=pmean, gn=ret_gn_g[l].reshape(1, RET_V), w_ro=w_ret_out[l].astype(BF16),
            w_fo_pad=jnp.pad(w_fox_out[l].reshape(H_FOX, DH_FOX, D_MODEL),
                             ((0, 0), (0, LANES - DH_FOX), (0, 0))).astype(BF16),
            w_o=w_o[l].astype(BF16), g2=norm2_g[l].reshape(1, D_MODEL), w_q=peer_w_q[l].astype(BF16),
            k1=peer_sub_k1[l].astype(BF16), k2=peer_sub_k2[l].astype(BF16), u=peer_u[l].astype(BF16),
            vt=peer_v[l].T.astype(BF16))
        s0_prompt = jnp.zeros((batch, N_PAIR, 2 * DK_RET, 2 * DV_RET), F32)
        xp, s_p, k_p, v_p, lf_p = _layer(xp, jnp.arange(seq), seq, s0_prompt, None, prm)
        xs, s_s, k_s, v_s, lf_s = _layer(xs, past + jnp.arange(dec_seq), dec_seq, _state_to_pair(state_ret[l]),
                                         (cache_fox_k[l], cache_fox_v[l], cache_fox_logf[l]), prm)
        for lst, val in zip(outs, (s_p, k_p, v_p, lf_p, s_s, k_s, v_s, lf_s)):
            lst.append(val)
    stacked = [jnp.stack(lst, axis=0) for lst in outs]
    return (xp, xs, *stacked)
```

```python
import functools
import math

import jax
import jax.numpy as jnp
import numpy as np
from jax import lax
from jax.experimental import pallas as pl
from jax.experimental.pallas import tpu as pltpu

D_MODEL = 1024
H_RET = 8
DK_RET = 64
DV_RET = 128
H_FOX = 8
DH_FOX = 64
ROPE_BASE = 10000.0
PEER_HEADS = 8
PEER_N_KEYS = 128
PEER_HALF = 128
PEER_TOPK = 16
EPS = 1e-6

RET_QK = H_RET * DK_RET
RET_V = H_RET * DV_RET
FOX_W = H_FOX * DH_FOX
N_PAIR = H_RET // 2
LANES = 128
VMEM_LIMIT = 56 * 1024 * 1024

BF16 = jnp.bfloat16
F32 = jnp.float32
LOG2E = math.log2(math.e)


def _cparams(sem):
    return pltpu.CompilerParams(dimension_semantics=sem, vmem_limit_bytes=VMEM_LIMIT)


def _const_spec(shape):
    nd = len(shape)
    return pl.BlockSpec(shape, lambda *_: (0,) * nd, pipeline_mode=pl.Buffered(1))


def _rms_rows(x, g):
    ms = jnp.mean(x * x, axis=-1, keepdims=True)
    return x * lax.rsqrt(ms + EPS) * g


def _split3(c):
    hi = c.astype(BF16).astype(F32)
    r = c - hi
    mid = r.astype(BF16).astype(F32)
    lo = (r - mid).astype(BF16).astype(F32)
    return hi, mid, lo


def _proj_ret_kernel(x_ref, g_ref, w_ref, cos_ref, sin_ref,
                     rq_ref, rk_ref, rv_ref, rgs_ref, sga_ref, sgb_ref):
    xn = _rms_rows(x_ref[...], g_ref[...]).astype(BF16)
    cosf = cos_ref[...]
    sinf = sin_ref[...]
    tm = xn.shape[0]
    lane = lax.broadcasted_iota(jnp.int32, (tm, RET_QK), 1)
    first_half = (lane % DK_RET) < (DK_RET // 2)

    def rotary(h):
        sw = jnp.where(first_half, pltpu.roll(h, RET_QK - DK_RET // 2, 1), pltpu.roll(h, DK_RET // 2, 1))
        parts = []
        for gidx in range(RET_QK // LANES):
            sl = slice(gidx * LANES, (gidx + 1) * LANES)
            parts.append(h[:, sl] * cosf + sw[:, sl] * sinf)
        return jnp.concatenate(parts, axis=-1)

    def proj(lo, width):
        return jnp.dot(xn, w_ref[:, lo:lo + width], preferred_element_type=F32)

    rq_ref[...] = rotary(proj(0, RET_QK)).astype(BF16)
    rk_ref[...] = (rotary(proj(RET_QK, RET_QK)) * (DK_RET ** -0.5)).astype(BF16)
    off = 2 * RET_QK
    rv_ref[...] = proj(off, RET_V).astype(BF16)
    rg = proj(off + RET_V, RET_V)
    rgs_ref[...] = (rg * jax.nn.sigmoid(rg)).astype(BF16)
    sga_ref[...] = jax.nn.sigmoid(proj(off + 2 * RET_V, D_MODEL)).astype(BF16)
    sgb_ref[...] = jax.nn.sigmoid(proj(off + 2 * RET_V + D_MODEL, D_MODEL)).astype(BF16)


def _proj_ret(x2d, g1, w_ret, cosf, sinf, tm):
    n = x2d.shape[0]
    nblk = n // tm
    tab_blocks = cosf.shape[0] // tm
    row = lambda w: pl.BlockSpec((tm, w), lambda i: (i, 0))
    tab = pl.BlockSpec((tm, LANES), lambda i: (i % tab_blocks, 0))
    widths = (RET_QK, RET_QK, RET_V, RET_V, D_MODEL, D_MODEL)
    return pl.pallas_call(
        _proj_ret_kernel,
        grid=(nblk,),
        in_specs=[row(D_MODEL), _const_spec((1, D_MODEL)), _const_spec(w_ret.shape), tab, tab],
        out_specs=[row(w) for w in widths],
        out_shape=[jax.ShapeDtypeStruct((n, w), BF16) for w in widths],
        compiler_params=_cparams(("arbitrary",)),
        name="proj_ret",
    )(x2d, g1, w_ret, cosf, sinf)


def _proj_fox_kernel(blocks_per_seq, x_ref, g_ref, w_ref, bf_ref, gq_ref, gkp_ref, gkd_ref, pmean_ref, tri_ref,
                     fk_ref, fv_ref, logf_ref, qp_ref, kp_ref, vt_ref, carry_ref):
    i = pl.program_id(0)
    xn = _rms_rows(x_ref[...], g_ref[...]).astype(BF16)
    tm = xn.shape[0]

    def proj(lo, width):
        return jnp.dot(xn, w_ref[:, lo:lo + width], preferred_element_type=F32)

    hk = proj(0, FOX_W)
    sq = hk * hk
    sq_hi = sq.astype(BF16)
    sq_lo = (sq - sq_hi.astype(F32)).astype(BF16)
    ms = (jnp.dot(sq_hi, pmean_ref[...], preferred_element_type=F32)
          + jnp.dot(sq_lo, pmean_ref[...], preferred_element_type=F32))
    fk_ref[...] = hk * lax.rsqrt(ms + EPS) * gkd_ref[...]
    fv_ref[...] = proj(FOX_W, FOX_W)

    z = proj(2 * FOX_W, LANES) + bf_ref[...]
    logf = jnp.minimum(z, 0.0) - jnp.log1p(jnp.exp(-jnp.abs(z)))
    logf_ref[...] = logf

    @pl.when(i % blocks_per_seq == 0)
    def _():
        carry_ref[...] = jnp.zeros_like(carry_ref)

    l_hi, l_mid, l_lo = _split3(logf)
    tri = tri_ref[...]
    c = (jnp.dot(tri, l_hi.astype(BF16), preferred_element_type=F32)
         + jnp.dot(tri, l_mid.astype(BF16), preferred_element_type=F32)
         + jnp.dot(tri, l_lo.astype(BF16), preferred_element_type=F32)) + carry_ref[...]
    carry_ref[...] = c[tm - 1:tm, :]

    lane = lax.broadcasted_iota(jnp.int32, (tm, LANES), 1)
    is_vec = lane < DH_FOX
    base = 2 * FOX_W + LANES
    for h in range(H_FOX):
        ch = jnp.broadcast_to(c[:, h:h + 1], (tm, LANES)) * LOG2E
        c_hi, c_mid, c_lo = _split3(ch)
        one = jnp.ones_like(ch)
        zero = jnp.zeros_like(ch)
        aug_q = jnp.where(lane == 64, c_hi, jnp.where(lane == 65, c_mid, jnp.where(
            lane == 66, c_lo, jnp.where(lane < 70, one, zero))))
        aug_k = jnp.where(lane < 67, one, jnp.where(lane == 67, -c_hi, jnp.where(
            lane == 68, -c_mid, jnp.where(lane == 69, -c_lo, zero))))
        hq = proj(base + h * LANES, LANES)
        hq = hq * lax.rsqrt(jnp.sum(hq * hq, axis=-1, keepdims=True) * (1.0 / DH_FOX) + EPS) * gq_ref[...]
        qp_ref[h] = jnp.where(is_vec, hq * (DH_FOX ** -0.5 * LOG2E), aug_q).astype(BF16)
        hkp = proj(base + (H_FOX + h) * LANES, LANES)
        hkp = hkp * lax.rsqrt(jnp.sum(hkp * hkp, axis=-1, keepdims=True) * (1.0 / DH_FOX) + EPS) * gkp_ref[...]
        kp_ref[h] = jnp.where(is_vec, hkp, aug_k).astype(BF16)
        hv = proj(base + (2 * H_FOX + h) * LANES, LANES)
        vt_ref[h] = jnp.where(is_vec, hv, one).T.astype(BF16)


def _proj_fox(x2d, g1, w_fox, bf_pad, gq_pad, gk_pad, gk_dense, pmean, tri, tm, blocks_per_seq):
    n = x2d.shape[0]
    nblk = n // tm
    row = lambda w: pl.BlockSpec((tm, w), lambda i: (i, 0))
    hm = pl.BlockSpec((H_FOX, tm, LANES), lambda i: (0, i, 0))
    return pl.pallas_call(
        functools.partial(_proj_fox_kernel, blocks_per_seq),
        grid=(nblk,),
        in_specs=[row(D_MODEL), _const_spec((1, D_MODEL)), _const_spec(w_fox.shape), _const_spec((1, LANES)),
                  _const_spec((1, LANES)), _const_spec((1, LANES)), _const_spec((1, FOX_W)),
                  _const_spec((FOX_W, FOX_W)), _const_spec((tm, tm))],
        out_specs=[row(FOX_W), row(FOX_W), row(LANES), hm, hm,
                   pl.BlockSpec((H_FOX, LANES, tm), lambda i: (0, 0, i))],
        out_shape=[jax.ShapeDtypeStruct((n, FOX_W), F32), jax.ShapeDtypeStruct((n, FOX_W), F32),
                   jax.ShapeDtypeStruct((n, LANES), F32), jax.ShapeDtypeStruct((H_FOX, n, LANES), BF16),
                   jax.ShapeDtypeStruct((H_FOX, n, LANES), BF16), jax.ShapeDtypeStruct((H_FOX, LANES, n), BF16)],
        scratch_shapes=[pltpu.VMEM((1, LANES), F32)],
        compiler_params=_cparams(("arbitrary",)),
        name="proj_fox",
    )(x2d, g1, w_fox, bf_pad, gq_pad, gk_pad, gk_dense, pmean, tri)


def _rotary_tables(pos):
    half = DK_RET // 2
    inv = ROPE_BASE ** (-jnp.arange(half, dtype=F32) / half)
    ang = pos.astype(F32)[:, None] * inv[None, :]
    cos = jnp.cos(ang)
    sin = jnp.sin(ang)
    cosf = jnp.concatenate([cos, cos, cos, cos], axis=-1)
    sinf = jnp.concatenate([-sin, sin, -sin, sin], axis=-1)
    return cosf, sinf


def _pad_heads(w, n_heads, dh):
    d = w.shape[0]
    w3 = w.reshape(d, n_heads, dh)
    return jnp.pad(w3, ((0, 0), (0, 0), (0, LANES - dh))).reshape(d, n_heads * LANES)


def _prep_mixer_params(w_in, b_f, fox_q_g, fox_k_g):
    sizes = (RET_QK, RET_QK, RET_V, RET_V, FOX_W, FOX_W, FOX_W, H_FOX, D_MODEL, D_MODEL)
    offs = np.concatenate([[0], np.cumsum(sizes)])
    col = lambda j: w_in[:, offs[j]:offs[j + 1]]
    w_ret = jnp.concatenate([col(0), col(1), col(2), col(3), col(8), col(9)], axis=1).astype(BF16)
    w_fz = jnp.pad(col(7), ((0, 0), (0, LANES - H_FOX)))
    w_fox = jnp.concatenate([col(5), col(6), w_fz, _pad_heads(col(4), H_FOX, DH_FOX),
                             _pad_heads(col(5), H_FOX, DH_FOX), _pad_heads(col(6), H_FOX, DH_FOX)],
                            axis=1).astype(BF16)
    bf_pad = jnp.pad(b_f, (0, LANES - H_FOX)).reshape(1, LANES)
    gq_pad = jnp.pad(fox_q_g, (0, LANES - DH_FOX)).reshape(1, LANES)
    gk_pad = jnp.pad(fox_k_g, (0, LANES - DH_FOX)).reshape(1, LANES)
    gk_dense = jnp.tile(fox_k_g, H_FOX).reshape(1, FOX_W)
    grp = np.arange(FOX_W) // DH_FOX
    pmean = jnp.asarray((grp[:, None] == grp[None, :]).astype(np.float32) / DH_FOX, dtype=BF16)
    return w_ret, w_fox, bf_pad, gq_pad, gk_pad, gk_dense, pmean


def _tri(tm):
    r = np.arange(tm)
    return jnp.asarray((r[None, :] <= r[:, None]).astype(np.float32), dtype=BF16)


def _retention_tables(chunk):
    log_gamma = jnp.log1p(-jnp.exp2(-5.0 - jnp.arange(H_RET, dtype=F32)))
    idx = jnp.arange(chunk, dtype=F32)
    diff = idx[:, None] - idx[None, :]
    dmat = jnp.where(diff >= 0, jnp.exp(jnp.maximum(diff, 0.0)[None] * log_gamma[:, None, None]), 0.0)
    lg_qk = jnp.repeat(log_gamma, DK_RET).reshape(N_PAIR, 1, 2 * DK_RET)
    qdec = jnp.exp((idx + 1.0)[None, :, None] * lg_qk)
    kdec = jnp.exp((chunk - 1.0 - idx)[None, :, None] * lg_qk)
    sdec = jnp.broadcast_to(jnp.exp(chunk * lg_qk).reshape(N_PAIR, 2 * DK_RET, 1), (N_PAIR, 2 * DK_RET, 2 * DV_RET))
    row_head = np.arange(2 * DK_RET) // DK_RET
    col_head = np.arange(2 * DV_RET) // DV_RET
    bmask = jnp.asarray((row_head[:, None] == col_head[None, :]).astype(np.float32))
    return dmat, qdec, kdec, sdec, bmask


def _retention_kernel(q_ref, k_ref, v_ref, s0_ref, dmat_ref, qdec_ref, kdec_ref, sdec_ref, bmask_ref,
                      o_ref, sout_ref, s_scr):
    c = pl.program_id(2)

    @pl.when(c == 0)
    def _():
        s_scr[...] = s0_ref[0, 0]

    q = q_ref[...]
    k = k_ref[...]
    v = v_ref[...]
    lane = lax.broadcasted_iota(jnp.int32, q.shape, 1)
    zero = jnp.zeros_like(q)
    nt = (((1,), (1,)), ((), ()))
    halves = []
    for j in range(2):
        qj = jnp.where((lane < DK_RET) == (j == 0), q, zero)
        sc = lax.dot_general(qj, k, nt, preferred_element_type=F32) * dmat_ref[j]
        halves.append(jnp.dot(sc.astype(BF16), v[:, j * DV_RET:(j + 1) * DV_RET], preferred_element_type=F32))
    inner = jnp.concatenate(halves, axis=-1)
    s = s_scr[...]
    qd = (q.astype(F32) * qdec_ref[0]).astype(BF16)
    cross = jnp.dot(qd, s.astype(BF16), preferred_element_type=F32)
    o_ref[...] = (inner + cross).astype(o_ref.dtype)
    kd = (k.astype(F32) * kdec_ref[0]).astype(BF16)
    upd = lax.dot_general(kd, v, (((0,), (0,)), ((), ())), preferred_element_type=F32)
    s_new = sdec_ref[0] * s + bmask_ref[...] * upd
    s_scr[...] = s_new

    @pl.when(c == pl.num_programs(2) - 1)
    def _():
        sout_ref[0, 0] = s_new


def _retention(rq, rk, rv, s0_pair, batch, seq, chunk):
    n = rq.shape[0]
    nchunk = seq // chunk
    dmat, qdec, kdec, sdec, bmask = _retention_tables(chunk)
    qk_spec = pl.BlockSpec((chunk, 2 * DK_RET), lambda b, p, c: (b * nchunk + c, p))
    v_spec = pl.BlockSpec((chunk, 2 * DV_RET), lambda b, p, c: (b * nchunk + c, p))
    s_spec = pl.BlockSpec((1, 1, 2 * DK_RET, 2 * DV_RET), lambda b, p, c: (b, p, 0, 0))
    pair_tab = lambda shape: pl.BlockSpec((1,) + shape, lambda b, p, c: (p, 0, 0))
    return pl.pallas_call(
        _retention_kernel,
        grid=(batch, N_PAIR, nchunk),
        in_specs=[qk_spec, qk_spec, v_spec, s_spec,
                  pl.BlockSpec((2, chunk, chunk), lambda b, p, c: (p, 0, 0)),
                  pair_tab((chunk, 2 * DK_RET)), pair_tab((chunk, 2 * DK_RET)),
                  pair_tab((2 * DK_RET, 2 * DV_RET)),
                  pl.BlockSpec((2 * DK_RET, 2 * DV_RET), lambda b, p, c: (0, 0))],
        out_specs=[v_spec, s_spec],
        out_shape=[jax.ShapeDtypeStruct((n, RET_V), BF16),
                   jax.ShapeDtypeStruct((batch, N_PAIR, 2 * DK_RET, 2 * DV_RET), F32)],
        scratch_shapes=[pltpu.VMEM((2 * DK_RET, 2 * DV_RET), F32)],
        compiler_params=_cparams(("arbitrary", "arbitrary", "arbitrary")),
        name="retention",
    )(rq, rk, rv, s0_pair, dmat, qdec, kdec, sdec, bmask)


NEG_BIG = -1e30
ATTN_LOOKAHEAD = 2


def _fox_attn_kernel(n_sub, qi_ref, kj_ref, q_ref, k_ref, vt_ref, o_ref, m_scr, acc_scr):
    s_idx = pl.program_id(2)
    qi = qi_ref[s_idx]
    kj = kj_ref[s_idx]
    tq = q_ref.shape[1]
    sub = k_ref.shape[1] // n_sub
    last_kj = (qi * tq) // (n_sub * sub)

    @pl.when(kj == 0)
    def _():
        m_scr[...] = jnp.full_like(m_scr, NEG_BIG)
        acc_scr[...] = jnp.zeros_like(acc_scr)

    def sweep(masked):
        q = q_ref[0]
        m = m_scr[...]
        acc = acc_scr[...]

        def scores(j):
            return lax.dot_general(k_ref[0, j * sub:(j + 1) * sub, :], q, (((1,), (1,)), ((), ())),
                                   preferred_element_type=F32)

        pending = [scores(j) for j in range(min(ATTN_LOOKAHEAD, n_sub))]
        for j in range(n_sub):
            st = pending.pop(0)
            if j + ATTN_LOOKAHEAD < n_sub:
                pending.append(scores(j + ATTN_LOOKAHEAD))
            if masked:
                kv_pos = kj * (n_sub * sub) + j * sub + lax.broadcasted_iota(jnp.int32, st.shape, 0)
                q_pos = qi * tq + lax.broadcasted_iota(jnp.int32, st.shape, 1)
                st = jnp.where(kv_pos <= q_pos, st, -jnp.inf)
            m_new = jnp.maximum(m, jnp.max(st, axis=0, keepdims=True))
            alpha = jnp.exp2(m - m_new)
            p = jnp.exp2(st - m_new).astype(BF16)
            acc = alpha * acc + jnp.dot(vt_ref[0, :, j * sub:(j + 1) * sub], p, preferred_element_type=F32)
            m = m_new
        m_scr[...] = m
        acc_scr[...] = acc

    @pl.when(kj < last_kj)
    def _():
        sweep(False)

    @pl.when(kj == last_kj)
    def _():
        sweep(True)
        acc = acc_scr[...]
        o_ref[0] = (acc / acc[DH_FOX:DH_FOX + 1, :]).T.astype(o_ref.dtype)


def _fox_attention(qp, kp, vt, batch, seq, tq, sub, n_sub):
    n = qp.shape[1]
    nq = seq // tq
    tkv = min(n_sub * sub, seq)
    n_sub = tkv // sub
    nkv = seq // tkv
    steps = [(i, j) for i in range(nq) for j in range((i * tq) // tkv + 1)]
    qi_tab = np.array([s[0] for s in steps], np.int32)
    kj_tab = np.array([s[1] for s in steps], np.int32)
    q_spec = pl.BlockSpec((1, tq, LANES), lambda h, b, s, qi, kj: (h, b * nq + qi[s], 0))
    k_spec = pl.BlockSpec((1, tkv, LANES), lambda h, b, s, qi, kj: (h, b * nkv + kj[s], 0))
    vt_spec = pl.BlockSpec((1, LANES, tkv), lambda h, b, s, qi, kj: (h, 0, b * nkv + kj[s]))
    return pl.pallas_call(
        functools.partial(_fox_attn_kernel, n_sub),
        grid_spec=pltpu.PrefetchScalarGridSpec(
            num_scalar_prefetch=2,
            grid=(H_FOX, batch, len(steps)),
            in_specs=[q_spec, k_spec, vt_spec],
            out_specs=q_spec,
            scratch_shapes=[pltpu.VMEM((1, tq), F32), pltpu.VMEM((LANES, tq), F32)],
        ),
        out_shape=jax.ShapeDtypeStruct((H_FOX, n, LANES), BF16),
        compiler_params=_cparams(("arbitrary", "arbitrary", "arbitrary")),
        name="fox_attention",
    )(jnp.asarray(qi_tab), jnp.asarray(kj_tab), qp, kp, vt)


def _fox_sample_kernel(qp_ref, kn_ref, vn_ref, kc_ref, vc_ref, lfc_ref, lfn_ref, tric_ref, trin_ref, place_ref,
                       o_ref):
    n_new = qp_ref.shape[1]
    past = kc_ref.shape[1]
    rows = H_FOX * n_new
    q_wide = jnp.concatenate([jnp.dot(qp_ref[h], place_ref[h], preferred_element_type=F32)
                              for h in range(H_FOX)], axis=0).astype(BF16)
    k_new = kn_ref[...].astype(BF16)
    v_new = vn_ref[...].astype(BF16)

    nt = (((1,), (1,)), ((), ()))
    s_c = lax.dot_general(q_wide, kc_ref[0].astype(BF16), nt, preferred_element_type=F32)
    s_n = lax.dot_general(q_wide, k_new, nt, preferred_element_type=F32)

    def cumsum_lanes(x, tri):
        hi, mid, lo = _split3(x)
        return (jnp.dot(hi.astype(BF16), tri, preferred_element_type=F32)
                + jnp.dot(mid.astype(BF16), tri, preferred_element_type=F32)
                + jnp.dot(lo.astype(BF16), tri, preferred_element_type=F32))

    c_c = cumsum_lanes(lfc_ref[0], tric_ref[...])
    c_n = (cumsum_lanes(lfn_ref[0], trin_ref[...]) + c_c[:, past - 1:past]) * LOG2E
    c_c = c_c * LOG2E

    def per_row(x):
        return jnp.concatenate([jnp.broadcast_to(x[h:h + 1], (n_new, x.shape[1])) for h in range(H_FOX)], axis=0)

    cn_rows = per_row(c_n)
    colq = lax.broadcasted_iota(jnp.int32, (rows, n_new), 1)
    rowq = lax.broadcasted_iota(jnp.int32, (rows, n_new), 0) % n_new
    c_q = jnp.sum(jnp.where(colq == rowq, cn_rows, 0.0), axis=-1, keepdims=True)

    s_c = s_c + c_q - per_row(c_c)
    s_n = jnp.where(colq <= rowq, s_n + c_q - cn_rows, -jnp.inf)
    m = jnp.maximum(jnp.max(s_c, axis=-1, keepdims=True), jnp.max(s_n, axis=-1, keepdims=True))
    p_c = jnp.exp2(s_c - m)
    p_n = jnp.exp2(s_n - m)
    denom = jnp.sum(p_c, axis=-1, keepdims=True) + jnp.sum(p_n, axis=-1, keepdims=True)
    pv = (jnp.dot(p_c.astype(BF16), vc_ref[0].astype(BF16), preferred_element_type=F32)
          + jnp.dot(p_n.astype(BF16), v_new, preferred_element_type=F32)) / denom
    pv = pv.astype(BF16)
    for h in range(H_FOX):
        o_ref[h] = lax.dot_general(pv[h * n_new:(h + 1) * n_new], place_ref[h], nt,
                                   preferred_element_type=F32).astype(o_ref.dtype)


def _fox_sample(qp, k_new, v_new, k_cache, v_cache, lf_cache_t, lf_new_t, n_streams, n_new):
    past = k_cache.shape[1]
    r = np.arange(past)
    tric = jnp.asarray((r[:, None] <= r[None, :]).astype(np.float32), dtype=BF16)
    rn = np.arange(n_new)
    trin = jnp.asarray((rn[:, None] <= rn[None, :]).astype(np.float32), dtype=BF16)
    place_np = np.zeros((H_FOX, LANES, FOX_W), np.float32)
    for h in range(H_FOX):
        place_np[h, np.arange(DH_FOX), h * DH_FOX + np.arange(DH_FOX)] = 1.0
    place = jnp.asarray(place_np, dtype=BF16)
    hm = pl.BlockSpec((H_FOX, n_new, LANES), lambda s: (0, s, 0))
    cache = pl.BlockSpec((1, past, FOX_W), lambda s: (s, 0, 0))
    return pl.pallas_call(
        _fox_sample_kernel,
        grid=(n_streams,),
        in_specs=[hm, pl.BlockSpec((n_new, FOX_W), lambda s: (s, 0)), pl.BlockSpec((n_new, FOX_W), lambda s: (s, 0)),
                  cache, cache,
                  pl.BlockSpec((1, H_FOX, past), lambda s: (s, 0, 0)),
                  pl.BlockSpec((1, H_FOX, n_new), lambda s: (s, 0, 0)),
                  _const_spec((past, past)), _const_spec((n_new, n_new)), _const_spec((H_FOX, LANES, FOX_W))],
        out_specs=hm,
        out_shape=jax.ShapeDtypeStruct((H_FOX, n_streams * n_new, LANES), BF16),
        compiler_params=_cparams(("arbitrary",)),
        name="fox_sample",
    )(qp, k_new, v_new, k_cache, v_cache, lf_cache_t, lf_new_t, tric, trin, place)


def _merge_kernel(x_ref, ro_ref, rgs_ref, fo_ref, sga_ref, sgb_ref, gn_ref, wro_ref, wfo_ref, wo_ref, g2_ref,
                  x1_ref, xn2_ref):
    ro = ro_ref[...].astype(F32)
    parts = []
    for h in range(H_RET):
        sl = slice(h * DV_RET, (h + 1) * DV_RET)
        parts.append(_rms_rows(ro[:, sl], gn_ref[:, sl]))
    gated = (rgs_ref[...].astype(F32) * jnp.concatenate(parts, axis=-1)).astype(BF16)
    ret_y = jnp.dot(gated, wro_ref[...], preferred_element_type=F32)
    fox_y = jnp.dot(fo_ref[0], wfo_ref[0], preferred_element_type=F32)
    for h in range(1, H_FOX):
        fox_y = fox_y + jnp.dot(fo_ref[h], wfo_ref[h], preferred_element_type=F32)
    mixed = sga_ref[...].astype(F32) * ret_y + sgb_ref[...].astype(F32) * fox_y
    x1 = x_ref[...] + jnp.dot(mixed.astype(BF16), wo_ref[...], preferred_element_type=F32)
    x1_ref[...] = x1
    xn2_ref[...] = _rms_rows(x1, g2_ref[...]).astype(BF16)


def _merge(x2d, ret_o, rgs, fox_o, sga, sgb, gn, w_ro, w_fo_pad, w_o, g2, tm):
    n = x2d.shape[0]
    row = lambda w: pl.BlockSpec((tm, w), lambda i: (i, 0))
    return pl.pallas_call(
        _merge_kernel,
        grid=(n // tm,),
        in_specs=[row(D_MODEL), row(RET_V), row(RET_V), pl.BlockSpec((H_FOX, tm, LANES), lambda i: (0, i, 0)),
                  row(D_MODEL), row(D_MODEL), _const_spec((1, RET_V)), _const_spec(w_ro.shape),
                  _const_spec(w_fo_pad.shape), _const_spec(w_o.shape), _const_spec((1, D_MODEL))],
        out_specs=[row(D_MODEL), row(D_MODEL)],
        out_shape=[jax.ShapeDtypeStruct((n, D_MODEL), F32), jax.ShapeDtypeStruct((n, D_MODEL), BF16)],
        compiler_params=_cparams(("arbitrary",)),
        name="merge",
    )(x2d, ret_o, rgs, fox_o, sga, sgb, gn, w_ro, w_fo_pad, w_o, g2)


_CAND_ROWS = ([(0, j) for j in range(16)] + [(i, j) for i in range(1, 8) for j in range(8)]
              + [(i, 0) for i in range(8, 16)])
_CAND_VALID = np.array([(i + 1) * (j + 1) <= PEER_TOPK for i, j in _CAND_ROWS])
_CAND_FLAT = np.array([i * PEER_TOPK + j for i, j in _CAND_ROWS], np.float32)


def _extract_topk(scores, order, k):
    cur = scores
    rank = jnp.full(scores.shape, float(k), F32)
    vals = []
    big = float(2 ** 20)
    for r in range(k):
        m = jnp.max(cur, axis=0, keepdims=True)
        first = jnp.min(jnp.where(cur == m, order, big), axis=0, keepdims=True)
        sel = order == first
        rank = jnp.where(sel, float(r), rank)
        cur = jnp.where(sel, -jnp.inf, cur)
        vals.append(m)
    return vals, rank


def _peer_route_kernel(xn_ref, wq_ref, k1_ref, k2_ref, cflat_ref, cvalid_ref,
                       n1_ref, e1_ref, r2_ref, e2_ref, q_scr):
    tm = xn_ref.shape[0]
    q_scr[...] = jnp.dot(xn_ref[...], wq_ref[...], preferred_element_type=F32).astype(BF16)
    key_iota = lax.broadcasted_iota(jnp.int32, (PEER_N_KEYS, tm), 0).astype(F32)
    cflat = jnp.broadcast_to(cflat_ref[...], (len(_CAND_ROWS), tm))
    cvalid = jnp.broadcast_to(cvalid_ref[...], (len(_CAND_ROWS), tm)) > 0.5
    nt = (((1,), (1,)), ((), ()))

    def head(h, carry):
        col = pl.multiple_of(h * (2 * PEER_HALF), 2 * PEER_HALF)
        s1 = lax.dot_general(k1_ref[...], q_scr[:, pl.ds(col, PEER_HALF)], nt, preferred_element_type=F32)
        s2 = lax.dot_general(k2_ref[...], q_scr[:, pl.ds(col + PEER_HALF, PEER_HALF)], nt,
                             preferred_element_type=F32)
        v1, rank1 = _extract_topk(s1, key_iota, PEER_TOPK)
        v2, rank2 = _extract_topk(s2, key_iota, PEER_TOPK)
        v2_lo = jnp.concatenate(v2[:8], axis=0)
        v2_all = jnp.concatenate(v2, axis=0)
        cand = jnp.concatenate([v1[0] + v2_all] + [v1[i] + v2_lo for i in range(1, 8)]
                               + [jnp.concatenate(v1[8:], axis=0) + v2[0]], axis=0)
        cand = jnp.where(cvalid, cand, -jnp.inf)
        _, crank = _extract_topk(cand, cflat, PEER_TOPK)
        picked = crank < float(PEER_TOPK)
        top = v1[0] + v2[0]
        z = jnp.sum(jnp.where(picked, jnp.exp(cand - top), 0.0), axis=0, keepdims=True)
        cnt = jnp.where(picked, 1.0, 0.0)
        n_rows = [jnp.sum(cnt[0:16], axis=0, keepdims=True)]
        n_rows += [jnp.sum(cnt[16 + 8 * (i - 1):16 + 8 * i], axis=0, keepdims=True) for i in range(1, 8)]
        n_rows += [cnt[72 + i:73 + i] for i in range(8)]
        n1 = jnp.zeros_like(s1)
        for i in range(PEER_TOPK):
            n1 = jnp.where(rank1 == float(i), n_rows[i], n1)
        n1_ref[h] = n1
        e1_ref[h] = jnp.exp(s1 - v1[0]) / z
        r2_ref[h] = rank2.astype(BF16)
        e2_ref[h] = jnp.exp(s2 - v2[0]).astype(BF16)
        return carry

    lax.fori_loop(0, PEER_HEADS, head, 0)


def _peer_route(xn2, w_q, k1, k2, tm):
    n = xn2.shape[0]
    ncand = len(_CAND_ROWS)
    cflat = jnp.asarray(_CAND_FLAT).reshape(ncand, 1)
    cvalid = jnp.asarray(_CAND_VALID.astype(np.float32)).reshape(ncand, 1)
    em = pl.BlockSpec((PEER_HEADS, PEER_N_KEYS, tm), lambda i: (0, 0, i))
    return pl.pallas_call(
        _peer_route_kernel,
        grid=(n // tm,),
        in_specs=[pl.BlockSpec((tm, D_MODEL), lambda i: (i, 0)), _const_spec(w_q.shape), _const_spec(k1.shape),
                  _const_spec(k2.shape), _const_spec((ncand, 1)), _const_spec((ncand, 1))],
        out_specs=[em, em, em, em],
        out_shape=[jax.ShapeDtypeStruct((PEER_HEADS, PEER_N_KEYS, n), dt) for dt in (F32, F32, BF16, BF16)],
        scratch_shapes=[pltpu.VMEM((tm, PEER_HEADS * 2 * PEER_HALF), BF16)],
        compiler_params=_cparams(("arbitrary",)),
        name="peer_route",
    )(xn2, w_q, k1, k2, cflat, cvalid)


PEER_EXPERT_SUB = 512
PEER_LOOKAHEAD = 2


def _peer_dense_kernel(xn_ref, x1_ref, u_ref, vt_ref, n1_ref, e1_ref, r2_ref, e2_ref, y_ref, acc_scr):
    e = pl.program_id(1)
    te = u_ref.shape[0]
    tm = xn_ref.shape[0]

    @pl.when(e == 0)
    def _():
        acc_scr[...] = jnp.zeros_like(acc_scr)

    sub = PEER_EXPERT_SUB
    n_sub = te // sub
    xn = xn_ref[...]
    zero = jnp.zeros((PEER_N_KEYS, tm), BF16)

    def pre_act(j):
        return lax.dot_general(u_ref[j * sub:(j + 1) * sub, :], xn, (((1,), (1,)), ((), ())),
                               preferred_element_type=F32)

    pending = [pre_act(j) for j in range(min(PEER_LOOKAHEAD, n_sub))]
    for j in range(n_sub):
        zt = pending.pop(0)
        if j + PEER_LOOKAHEAD < n_sub:
            pending.append(pre_act(j + PEER_LOOKAHEAD))
        act = (0.5 * zt * (1.0 + lax.erf(zt * (2.0 ** -0.5)))).astype(BF16)
        gates = []
        for a_loc in range(sub // PEER_N_KEYS):
            a = e * (te // PEER_N_KEYS) + j * (sub // PEER_N_KEYS) + a_loc
            g = zero
            for h in range(PEER_HEADS):
                n1 = n1_ref[h, pl.ds(a, 1), :].astype(BF16)
                e1 = e1_ref[h, pl.ds(a, 1), :].astype(BF16)
                g = g + jnp.where(r2_ref[h] < n1, e2_ref[h], zero) * e1
            gates.append(g)
        w = act * jnp.concatenate(gates, axis=0)
        acc_scr[...] += jnp.dot(vt_ref[:, j * sub:(j + 1) * sub], w, preferred_element_type=F32)

    @pl.when(e == pl.num_programs(1) - 1)
    def _():
        y_ref[...] = x1_ref[...] + acc_scr[...].T


def _peer_dense(xn2, x1, u_bf, vt_bf, n1, e1, r2, e2, tm, te):
    n = xn2.shape[0]
    n_exp = u_bf.shape[0]
    em = pl.BlockSpec((PEER_HEADS, PEER_N_KEYS, tm), lambda i, e: (0, 0, i))
    return pl.pallas_call(
        _peer_dense_kernel,
        grid=(n // tm, n_exp // te),
        in_specs=[pl.BlockSpec((tm, D_MODEL), lambda i, e: (i, 0)), pl.BlockSpec((tm, D_MODEL), lambda i, e: (i, 0)),
                  pl.BlockSpec((te, D_MODEL), lambda i, e: (e, 0)), pl.BlockSpec((D_MODEL, te), lambda i, e: (0, e)),
                  em, em, em, em],
        out_specs=pl.BlockSpec((tm, D_MODEL), lambda i, e: (i, 0)),
        out_shape=jax.ShapeDtypeStruct((n, D_MODEL), F32),
        scratch_shapes=[pltpu.VMEM((D_MODEL, tm), F32)],
        compiler_params=_cparams(("arbitrary", "arbitrary")),
        name="peer_dense",
    )(xn2, x1, u_bf, vt_bf, n1, e1, r2, e2)


def _state_to_pair(s):
    b = s.shape[0]
    s5 = s.reshape(b, N_PAIR, 2, DK_RET, DV_RET)
    z = jnp.zeros_like(s5[:, :, 0])
    top = jnp.concatenate([s5[:, :, 0], z], axis=-1)
    bot = jnp.concatenate([z, s5[:, :, 1]], axis=-1)
    return jnp.concatenate([top, bot], axis=-2)


def _pair_to_state(sp):
    b = sp.shape[0]
    a = sp[:, :, :DK_RET, :DV_RET]
    c = sp[:, :, DK_RET:, DV_RET:]
    return jnp.stack([a, c], axis=2).reshape(b, H_RET, DK_RET, DV_RET)


ROW_TILE = 512
RET_CHUNK = 256
ATTN_TILE = 512
ATTN_KV_SUB = 512
ATTN_KV_SUBTILES = 4
PEER_ROUTE_TILE = 256
PEER_TOKEN_TILE = 512
PEER_EXPERT_TILE = 2048


def _layer(x, pos, seq, s0_pair, fox_cache, prm):
    batch = x.shape[0]
    n = batch * seq
    x2d = x.reshape(n, D_MODEL)
    tm = min(ROW_TILE, n)
    cosf, sinf = _rotary_tables(pos)
    if fox_cache is not None:
        cosf = jnp.tile(cosf, (batch, 1))
        sinf = jnp.tile(sinf, (batch, 1))
    rq, rk, rv, rgs, sga, sgb = _proj_ret(x2d, prm['g1'], prm['w_ret'], cosf, sinf, tm)
    blocks_per_seq = max(seq // tm, 1)
    fk, fv, logf_pad, qp, kp, vt = _proj_fox(x2d, prm['g1'], prm['w_fox'], prm['bf_pad'], prm['gq_pad'],
                                             prm['gk_pad'], prm['gk_dense'], prm['pmean'], _tri(tm), tm,
                                             blocks_per_seq)
    logf = logf_pad[:, :H_FOX]
    if fox_cache is None:
        ret_o, s_pair = _retention(rq, rk, rv, s0_pair, batch, seq, min(RET_CHUNK, seq))
        fox_o = _fox_attention(qp, kp, vt, batch, seq, min(ATTN_TILE, seq), min(ATTN_KV_SUB, seq),
                               ATTN_KV_SUBTILES)
    else:
        ret_o, s_pair = _retention(rq, rk, rv, s0_pair, batch, seq, seq)
        k_c, v_c, lf_c = fox_cache
        past = k_c.shape[1]
        fox_o = _fox_sample(qp, fk, fv, k_c.reshape(batch, past, FOX_W), v_c.reshape(batch, past, FOX_W),
                            lf_c.transpose(0, 2, 1), logf.reshape(batch, seq, H_FOX).transpose(0, 2, 1), batch, seq)
    x1, xn2 = _merge(x2d, ret_o, rgs, fox_o, sga, sgb, prm['gn'], prm['w_ro'], prm['w_fo_pad'], prm['w_o'],
                     prm['g2'], tm)
    n1, e1, r2, e2 = _peer_route(xn2, prm['w_q'], prm['k1'], prm['k2'], min(PEER_ROUTE_TILE, n))
    y = _peer_dense(xn2, x1, prm['u'], prm['vt'], n1, e1, r2, e2, min(PEER_TOKEN_TILE, n), PEER_EXPERT_TILE)
    return (y.reshape(batch, seq, D_MODEL), _pair_to_state(s_pair), fk.reshape(batch, seq, H_FOX, DH_FOX),
            fv.reshape(batch, seq, H_FOX, DH_FOX), logf.reshape(batch, seq, H_FOX))


def kernel(x_prompt, x_sample, state_ret, cache_fox_k, cache_fox_v, cache_fox_logf, norm1_g, w_in, b_f, fox_q_g,
           fox_k_g, ret_gn_g, w_ret_out, w_fox_out, w_o, norm2_g, peer_w_q, peer_sub_k1, peer_sub_k2, peer_u,
           peer_v):
    depth = w_in.shape[0]
    batch, seq, _ = x_prompt.shape
    dec_batch, dec_seq, _ = x_sample.shape
    past = cache_fox_k.shape[2]
    xp, xs = x_prompt, x_sample
    outs = [[] for _ in range(8)]
    for l in range(depth):
        w_ret, w_fox, bf_pad, gq_pad, gk_pad, gk_dense, pmean = _prep_mixer_params(w_in[l], b_f[l], fox_q_g[l],
                                                                                  fox_k_g[l])
        prm = dict(
            g1=norm1_g[l].reshape(1, D_MODEL), w_ret=w_ret, w_fox=w_fox, bf_pad=bf_pad, gq_pad=gq_pad, gk_pad=gk_pad,
            gk_dense=gk_dense, pmean=pmean, gn=ret_gn_g[l].reshape(1, RET_V), w_ro=w_ret_out[l].astype(BF16),
            w_fo_pad=jnp.pad(w_fox_out[l].reshape(H_FOX, DH_FOX, D_MODEL),
                             ((0, 0), (0, LANES - DH_FOX), (0, 0))).astype(BF16),
            w_o=w_o[l].astype(BF16), g2=norm2_g[l].reshape(1, D_MODEL), w_q=peer_w_q[l].astype(BF16),
            k1=peer_sub_k1[l].astype(BF16), k2=peer_sub_k2[l].astype(BF16), u=peer_u[l].astype(BF16),
            vt=peer_v[l].T.astype(BF16))
        s0_prompt = jnp.zeros((batch, N_PAIR, 2 * DK_RET, 2 * DV_RET), F32)
        xp, s_p, k_p, v_p, lf_p = _layer(xp, jnp.arange(seq), seq, s0_prompt, None, prm)
        xs, s_s, k_s, v_s, lf_s = _layer(xs, past + jnp.arange(dec_seq), dec_seq, _state_to_pair(state_ret[l]),
                                         (cache_fox_k[l], cache_fox_v[l], cache_fox_logf[l]), prm)
        for lst, val in zip(outs, (s_p, k_p, v_p, lf_p, s_s, k_s, v_s, lf_s)):
            lst.append(val)
    stacked = [jnp.stack(lst, axis=0) for lst in outs]
    return (xp, xs, *stacked)
```

```python
import functools
import math

import jax
import jax.numpy as jnp
import numpy as np
from jax import lax
from jax.experimental import pallas as pl
from jax.experimental.pallas import tpu as pltpu

D_MODEL = 1024
H_RET = 8
DK_RET = 64
DV_RET = 128
H_FOX = 8
DH_FOX = 64
ROPE_BASE = 10000.0
PEER_HEADS = 8
PEER_N_KEYS = 128
PEER_HALF = 128
PEER_TOPK = 16
EPS = 1e-6

RET_QK = H_RET * DK_RET
RET_V = H_RET * DV_RET
FOX_W = H_FOX * DH_FOX
N_PAIR = H_RET // 2
LANES = 128
VMEM_LIMIT = 56 * 1024 * 1024

BF16 = jnp.bfloat16
F32 = jnp.float32
LOG2E = math.log2(math.e)


def _cparams(sem):
    return pltpu.CompilerParams(dimension_semantics=sem, vmem_limit_bytes=VMEM_LIMIT)


def _const_spec(shape):
    nd = len(shape)
    return pl.BlockSpec(shape, lambda *_: (0,) * nd, pipeline_mode=pl.Buffered(1))


def _rms_rows(x, g):
    ms = jnp.mean(x * x, axis=-1, keepdims=True)
    return x * lax.rsqrt(ms + EPS) * g


def _split3(c):
    hi = c.astype(BF16).astype(F32)
    r = c - hi
    mid = r.astype(BF16).astype(F32)
    lo = (r - mid).astype(BF16).astype(F32)
    return hi, mid, lo


def _proj_ret_kernel(x_ref, g_ref, w_ref, cos_ref, sin_ref,
                     rq_ref, rk_ref, rv_ref, rgs_ref, sga_ref, sgb_ref):
    xn = _rms_rows(x_ref[...], g_ref[...]).astype(BF16)
    cosf = cos_ref[...]
    sinf = sin_ref[...]
    tm = xn.shape[0]
    lane = lax.broadcasted_iota(jnp.int32, (tm, RET_QK), 1)
    first_half = (lane % DK_RET) < (DK_RET // 2)

    def rotary(h):
        sw = jnp.where(first_half, pltpu.roll(h, RET_QK - DK_RET // 2, 1), pltpu.roll(h, DK_RET // 2, 1))
        parts = []
        for gidx in range(RET_QK // LANES):
            sl = slice(gidx * LANES, (gidx + 1) * LANES)
            parts.append(h[:, sl] * cosf + sw[:, sl] * sinf)
        return jnp.concatenate(parts, axis=-1)

    def proj(lo, width):
        return jnp.dot(xn, w_ref[:, lo:lo + width], preferred_element_type=F32)

    rq_ref[...] = rotary(proj(0, RET_QK)).astype(BF16)
    rk_ref[...] = (rotary(proj(RET_QK, RET_QK)) * (DK_RET ** -0.5)).astype(BF16)
    off = 2 * RET_QK
    rv_ref[...] = proj(off, RET_V).astype(BF16)
    rg = proj(off + RET_V, RET_V)
    rgs_ref[...] = (rg * jax.nn.sigmoid(rg)).astype(BF16)
    sga_ref[...] = jax.nn.sigmoid(proj(off + 2 * RET_V, D_MODEL)).astype(BF16)
    sgb_ref[...] = jax.nn.sigmoid(proj(off + 2 * RET_V + D_MODEL, D_MODEL)).astype(BF16)


def _proj_ret(x2d, g1, w_ret, cosf, sinf, tm):
    n = x2d.shape[0]
    nblk = n // tm
    tab_blocks = cosf.shape[0] // tm
    row = lambda w: pl.BlockSpec((tm, w), lambda i: (i, 0))
    tab = pl.BlockSpec((tm, LANES), lambda i: (i % tab_blocks, 0))
    widths = (RET_QK, RET_QK, RET_V, RET_V, D_MODEL, D_MODEL)
    return pl.pallas_call(
        _proj_ret_kernel,
        grid=(nblk,),
        in_specs=[row(D_MODEL), _const_spec((1, D_MODEL)), _const_spec(w_ret.shape), tab, tab],
        out_specs=[row(w) for w in widths],
        out_shape=[jax.ShapeDtypeStruct((n, w), BF16) for w in widths],
        compiler_params=_cparams(("arbitrary",)),
        name="proj_ret",
    )(x2d, g1, w_ret, cosf, sinf)


def _proj_fox_kernel(blocks_per_seq, x_ref, g_ref, w_ref, bf_ref, gq_ref, gkp_ref, gkd_ref, pmean_ref, tri_ref,
                     fk_ref, fv_ref, logf_ref, qp_ref, kp_ref, vt_ref, carry_ref):
    i = pl.program_id(0)
    xn = _rms_rows(x_ref[...], g_ref[...]).astype(BF16)
    tm = xn.shape[0]

    def proj(lo, width):
        return jnp.dot(xn, w_ref[:, lo:lo + width], preferred_element_type=F32)

    hk = proj(0, FOX_W)
    sq = hk * hk
    sq_hi = sq.astype(BF16)
    sq_lo = (sq - sq_hi.astype(F32)).astype(BF16)
    ms = (jnp.dot(sq_hi, pmean_ref[...], preferred_element_type=F32)
          + jnp.dot(sq_lo, pmean_ref[...], preferred_element_type=F32))
    fk_ref[...] = hk * lax.rsqrt(ms + EPS) * gkd_ref[...]
    fv_ref[...] = proj(FOX_W, FOX_W)

    z = proj(2 * FOX_W, LANES) + bf_ref[...]
    logf = jnp.minimum(z, 0.0) - jnp.log1p(jnp.exp(-jnp.abs(z)))
    logf_ref[...] = logf

    @pl.when(i % blocks_per_seq == 0)
    def _():
        carry_ref[...] = jnp.zeros_like(carry_ref)

    l_hi, l_mid, l_lo = _split3(logf)
    tri = tri_ref[...]
    c = (jnp.dot(tri, l_hi.astype(BF16), preferred_element_type=F32)
         + jnp.dot(tri, l_mid.astype(BF16), preferred_element_type=F32)
         + jnp.dot(tri, l_lo.astype(BF16), preferred_element_type=F32)) + carry_ref[...]
    carry_ref[...] = c[tm - 1:tm, :]

    lane = lax.broadcasted_iota(jnp.int32, (tm, LANES), 1)
    is_vec = lane < DH_FOX
    base = 2 * FOX_W + LANES
    for h in range(H_FOX):
        ch = jnp.broadcast_to(c[:, h:h + 1], (tm, LANES)) * LOG2E
        c_hi, c_mid, c_lo = _split3(ch)
        one = jnp.ones_like(ch)
        zero = jnp.zeros_like(ch)
        aug_q = jnp.where(lane == 64, c_hi, jnp.where(lane == 65, c_mid, jnp.where(
            lane == 66, c_lo, jnp.where(lane < 70, one, zero))))
        aug_k = jnp.where(lane < 67, one, jnp.where(lane == 67, -c_hi, jnp.where(
            lane == 68, -c_mid, jnp.where(lane == 69, -c_lo, zero))))
        hq = proj(base + h * LANES, LANES)
        hq = hq * lax.rsqrt(jnp.sum(hq * hq, axis=-1, keepdims=True) * (1.0 / DH_FOX) + EPS) * gq_ref[...]
        qp_ref[h] = jnp.where(is_vec, hq * (DH_FOX ** -0.5 * LOG2E), aug_q).astype(BF16)
        hkp = proj(base + (H_FOX + h) * LANES, LANES)
        hkp = hkp * lax.rsqrt(jnp.sum(hkp * hkp, axis=-1, keepdims=True) * (1.0 / DH_FOX) + EPS) * gkp_ref[...]
        kp_ref[h] = jnp.where(is_vec, hkp, aug_k).astype(BF16)
        hv = proj(base + (2 * H_FOX + h) * LANES, LANES)
        vt_ref[h] = jnp.where(is_vec, hv, one).T[:vt_ref.shape[1]].astype(BF16)


def _proj_fox(x2d, g1, w_fox, bf_pad, gq_pad, gk_pad, gk_dense, pmean, tri, tm, blocks_per_seq):
    n = x2d.shape[0]
    nblk = n // tm
    row = lambda w: pl.BlockSpec((tm, w), lambda i: (i, 0))
    hm = pl.BlockSpec((H_FOX, tm, LANES), lambda i: (0, i, 0))
    return pl.pallas_call(
        functools.partial(_proj_fox_kernel, blocks_per_seq),
        grid=(nblk,),
        in_specs=[row(D_MODEL), _const_spec((1, D_MODEL)), _const_spec(w_fox.shape), _const_spec((1, LANES)),
                  _const_spec((1, LANES)), _const_spec((1, LANES)), _const_spec((1, FOX_W)),
                  _const_spec((FOX_W, FOX_W)), _const_spec((tm, tm))],
        out_specs=[row(FOX_W), row(FOX_W), row(LANES), hm, hm,
                   pl.BlockSpec((H_FOX, V_ROWS, tm), lambda i: (0, 0, i))],
        out_shape=[jax.ShapeDtypeStruct((n, FOX_W), F32), jax.ShapeDtypeStruct((n, FOX_W), F32),
                   jax.ShapeDtypeStruct((n, LANES), F32), jax.ShapeDtypeStruct((H_FOX, n, LANES), BF16),
                   jax.ShapeDtypeStruct((H_FOX, n, LANES), BF16), jax.ShapeDtypeStruct((H_FOX, V_ROWS, n), BF16)],
        scratch_shapes=[pltpu.VMEM((1, LANES), F32)],
        compiler_params=_cparams(("arbitrary",)),
        name="proj_fox",
    )(x2d, g1, w_fox, bf_pad, gq_pad, gk_pad, gk_dense, pmean, tri)


def _rotary_tables(pos):
    half = DK_RET // 2
    inv = ROPE_BASE ** (-jnp.arange(half, dtype=F32) / half)
    ang = pos.astype(F32)[:, None] * inv[None, :]
    cos = jnp.cos(ang)
    sin = jnp.sin(ang)
    cosf = jnp.concatenate([cos, cos, cos, cos], axis=-1)
    sinf = jnp.concatenate([-sin, sin, -sin, sin], axis=-1)
    return cosf, sinf


def _pad_heads(w, n_heads, dh):
    d = w.shape[0]
    w3 = w.reshape(d, n_heads, dh)
    return jnp.pad(w3, ((0, 0), (0, 0), (0, LANES - dh))).reshape(d, n_heads * LANES)


def _prep_mixer_params(w_in, b_f, fox_q_g, fox_k_g):
    sizes = (RET_QK, RET_QK, RET_V, RET_V, FOX_W, FOX_W, FOX_W, H_FOX, D_MODEL, D_MODEL)
    offs = np.concatenate([[0], np.cumsum(sizes)])
    col = lambda j: w_in[:, offs[j]:offs[j + 1]]
    w_ret = jnp.concatenate([col(0), col(1), col(2), col(3), col(8), col(9)], axis=1).astype(BF16)
    w_fz = jnp.pad(col(7), ((0, 0), (0, LANES - H_FOX)))
    w_fox = jnp.concatenate([col(5), col(6), w_fz, _pad_heads(col(4), H_FOX, DH_FOX),
                             _pad_heads(col(5), H_FOX, DH_FOX), _pad_heads(col(6), H_FOX, DH_FOX)],
                            axis=1).astype(BF16)
    bf_pad = jnp.pad(b_f, (0, LANES - H_FOX)).reshape(1, LANES)
    gq_pad = jnp.pad(fox_q_g, (0, LANES - DH_FOX)).reshape(1, LANES)
    gk_pad = jnp.pad(fox_k_g, (0, LANES - DH_FOX)).reshape(1, LANES)
    gk_dense = jnp.tile(fox_k_g, H_FOX).reshape(1, FOX_W)
    grp = np.arange(FOX_W) // DH_FOX
    pmean = jnp.asarray((grp[:, None] == grp[None, :]).astype(np.float32) / DH_FOX, dtype=BF16)
    return w_ret, w_fox, bf_pad, gq_pad, gk_pad, gk_dense, pmean


def _tri(tm):
    r = np.arange(tm)
    return jnp.asarray((r[None, :] <= r[:, None]).astype(np.float32), dtype=BF16)


def _retention_tables(chunk):
    log_gamma = jnp.log1p(-jnp.exp2(-5.0 - jnp.arange(H_RET, dtype=F32)))
    idx = jnp.arange(chunk, dtype=F32)
    diff = idx[:, None] - idx[None, :]
    dmat = jnp.where(diff >= 0, jnp.exp(jnp.maximum(diff, 0.0)[None] * log_gamma[:, None, None]), 0.0)
    lg_qk = jnp.repeat(log_gamma, DK_RET).reshape(N_PAIR, 1, 2 * DK_RET)
    qdec = jnp.exp((idx + 1.0)[None, :, None] * lg_qk)
    kdec = jnp.exp((chunk - 1.0 - idx)[None, :, None] * lg_qk)
    sdec = jnp.broadcast_to(jnp.exp(chunk * lg_qk).reshape(N_PAIR, 2 * DK_RET, 1), (N_PAIR, 2 * DK_RET, 2 * DV_RET))
    row_head = np.arange(2 * DK_RET) // DK_RET
    col_head = np.arange(2 * DV_RET) // DV_RET
    bmask = jnp.asarray((row_head[:, None] == col_head[None, :]).astype(np.float32))
    return dmat, qdec, kdec, sdec, bmask


def _retention_kernel(q_ref, k_ref, v_ref, s0_ref, dmat_ref, qdec_ref, kdec_ref, sdec_ref, bmask_ref,
                      o_ref, sout_ref, s_scr):
    c = pl.program_id(2)

    @pl.when(c == 0)
    def _():
        s_scr[...] = s0_ref[0, 0]

    q = q_ref[...]
    k = k_ref[...]
    v = v_ref[...]
    lane = lax.broadcasted_iota(jnp.int32, q.shape, 1)
    zero = jnp.zeros_like(q)
    nt = (((1,), (1,)), ((), ()))
    halves = []
    for j in range(2):
        qj = jnp.where((lane < DK_RET) == (j == 0), q, zero)
        sc = lax.dot_general(qj, k, nt, preferred_element_type=F32) * dmat_ref[j]
        halves.append(jnp.dot(sc.astype(BF16), v[:, j * DV_RET:(j + 1) * DV_RET], preferred_element_type=F32))
    inner = jnp.concatenate(halves, axis=-1)
    s = s_scr[...]
    qd = (q.astype(F32) * qdec_ref[0]).astype(BF16)
    cross = jnp.dot(qd, s.astype(BF16), preferred_element_type=F32)
    o_ref[...] = (inner + cross).astype(o_ref.dtype)
    kd = (k.astype(F32) * kdec_ref[0]).astype(BF16)
    upd = lax.dot_general(kd, v, (((0,), (0,)), ((), ())), preferred_element_type=F32)
    s_new = sdec_ref[0] * s + bmask_ref[...] * upd
    s_scr[...] = s_new

    @pl.when(c == pl.num_programs(2) - 1)
    def _():
        sout_ref[0, 0] = s_new


def _retention(rq, rk, rv, s0_pair, batch, seq, chunk):
    n = rq.shape[0]
    nchunk = seq // chunk
    dmat, qdec, kdec, sdec, bmask = _retention_tables(chunk)
    qk_spec = pl.BlockSpec((chunk, 2 * DK_RET), lambda b, p, c: (b * nchunk + c, p))
    v_spec = pl.BlockSpec((chunk, 2 * DV_RET), lambda b, p, c: (b * nchunk + c, p))
    s_spec = pl.BlockSpec((1, 1, 2 * DK_RET, 2 * DV_RET), lambda b, p, c: (b, p, 0, 0))
    pair_tab = lambda shape: pl.BlockSpec((1,) + shape, lambda b, p, c: (p, 0, 0))
    return pl.pallas_call(
        _retention_kernel,
        grid=(batch, N_PAIR, nchunk),
        in_specs=[qk_spec, qk_spec, v_spec, s_spec,
                  pl.BlockSpec((2, chunk, chunk), lambda b, p, c: (p, 0, 0)),
                  pair_tab((chunk, 2 * DK_RET)), pair_tab((chunk, 2 * DK_RET)),
                  pair_tab((2 * DK_RET, 2 * DV_RET)),
                  pl.BlockSpec((2 * DK_RET, 2 * DV_RET), lambda b, p, c: (0, 0))],
        out_specs=[v_spec, s_spec],
        out_shape=[jax.ShapeDtypeStruct((n, RET_V), BF16),
                   jax.ShapeDtypeStruct((batch, N_PAIR, 2 * DK_RET, 2 * DV_RET), F32)],
        scratch_shapes=[pltpu.VMEM((2 * DK_RET, 2 * DV_RET), F32)],
        compiler_params=_cparams(("arbitrary", "arbitrary", "arbitrary")),
        name="retention",
    )(rq, rk, rv, s0_pair, dmat, qdec, kdec, sdec, bmask)


NEG_BIG = -1e30
ATTN_LOOKAHEAD = 2


V_ROWS = DH_FOX + 16


def _fox_attn_kernel(n_sub, qi_ref, kj_ref, q_ref, k_ref, vt_ref, o_ref, m_scr, acc_scr):
    s_idx = pl.program_id(2)
    qi = qi_ref[s_idx]
    kj = kj_ref[s_idx]
    n_heads = q_ref.shape[0]
    tq = q_ref.shape[1]
    sub = k_ref.shape[1] // n_sub
    last_kj = (qi * tq) // (n_sub * sub)

    @pl.when(kj == 0)
    def _():
        m_scr[...] = jnp.full_like(m_scr, NEG_BIG)
        acc_scr[...] = jnp.zeros_like(acc_scr)

    def sweep(masked):
        qs = [q_ref[g] for g in range(n_heads)]
        ms = [m_scr[g] for g in range(n_heads)]
        accs = [acc_scr[g] for g in range(n_heads)]

        def scores(j):
            return [lax.dot_general(k_ref[g, j * sub:(j + 1) * sub, :], qs[g], (((1,), (1,)), ((), ())),
                                    preferred_element_type=F32) for g in range(n_heads)]

        pending = [scores(j) for j in range(min(ATTN_LOOKAHEAD, n_sub))]
        for j in range(n_sub):
            sts = pending.pop(0)
            if j + ATTN_LOOKAHEAD < n_sub:
                pending.append(scores(j + ATTN_LOOKAHEAD))
            for g in range(n_heads):
                st = sts[g]
                if masked:
                    kv_pos = kj * (n_sub * sub) + j * sub + lax.broadcasted_iota(jnp.int32, st.shape, 0)
                    q_pos = qi * tq + lax.broadcasted_iota(jnp.int32, st.shape, 1)
                    st = jnp.where(kv_pos <= q_pos, st, -jnp.inf)
                m_new = jnp.maximum(ms[g], jnp.max(st, axis=0, keepdims=True))
                alpha = jnp.exp2(ms[g] - m_new)
                p = jnp.exp2(st - m_new).astype(BF16)
                accs[g] = alpha * accs[g] + jnp.dot(vt_ref[g, :, j * sub:(j + 1) * sub], p,
                                                    preferred_element_type=F32)
                ms[g] = m_new
        for g in range(n_heads):
            m_scr[g] = ms[g]
            acc_scr[g] = accs[g]

    @pl.when(kj < last_kj)
    def _():
        sweep(False)

    @pl.when(kj == last_kj)
    def _():
        sweep(True)
        for g in range(n_heads):
            acc = acc_scr[g]
            o = acc / acc[DH_FOX:DH_FOX + 1, :]
            o = jnp.concatenate([o, jnp.zeros((LANES - V_ROWS, tq), F32)], axis=0)
            o_ref[g] = o.T.astype(o_ref.dtype)


def _fox_attention(qp, kp, vt, batch, seq, tq, sub, n_sub, n_heads):
    n = qp.shape[1]
    nq = seq // tq
    tkv = min(n_sub * sub, seq)
    n_sub = tkv // sub
    nkv = seq // tkv
    steps = [(i, j) for i in range(nq) for j in range((i * tq) // tkv + 1)]
    qi_tab = np.array([s[0] for s in steps], np.int32)
    kj_tab = np.array([s[1] for s in steps], np.int32)
    q_spec = pl.BlockSpec((n_heads, tq, LANES), lambda h, b, s, qi, kj: (h, b * nq + qi[s], 0))
    k_spec = pl.BlockSpec((n_heads, tkv, LANES), lambda h, b, s, qi, kj: (h, b * nkv + kj[s], 0))
    vt_spec = pl.BlockSpec((n_heads, V_ROWS, tkv), lambda h, b, s, qi, kj: (h, 0, b * nkv + kj[s]))
    return pl.pallas_call(
        functools.partial(_fox_attn_kernel, n_sub),
        grid_spec=pltpu.PrefetchScalarGridSpec(
            num_scalar_prefetch=2,
            grid=(H_FOX // n_heads, batch, len(steps)),
            in_specs=[q_spec, k_spec, vt_spec],
            out_specs=q_spec,
            scratch_shapes=[pltpu.VMEM((n_heads, 1, tq), F32), pltpu.VMEM((n_heads, V_ROWS, tq), F32)],
        ),
        out_shape=jax.ShapeDtypeStruct((H_FOX, n, LANES), BF16),
        compiler_params=_cparams(("arbitrary", "arbitrary", "arbitrary")),
        name="fox_attention",
    )(jnp.asarray(qi_tab), jnp.asarray(kj_tab), qp, kp, vt)


def _fox_sample_kernel(qp_ref, kn_ref, vn_ref, kc_ref, vc_ref, lfc_ref, lfn_ref, tric_ref, trin_ref, place_ref,
                       o_ref):
    n_new = qp_ref.shape[1]
    past = kc_ref.shape[1]
    rows = H_FOX * n_new
    q_wide = jnp.concatenate([jnp.dot(qp_ref[h], place_ref[h], preferred_element_type=F32)
                              for h in range(H_FOX)], axis=0).astype(BF16)
    k_new = kn_ref[...].astype(BF16)
    v_new = vn_ref[...].astype(BF16)

    nt = (((1,), (1,)), ((), ()))
    s_c = lax.dot_general(q_wide, kc_ref[0].astype(BF16), nt, preferred_element_type=F32)
    s_n = lax.dot_general(q_wide, k_new, nt, preferred_element_type=F32)

    def cumsum_lanes(x, tri):
        hi, mid, lo = _split3(x)
        return (jnp.dot(hi.astype(BF16), tri, preferred_element_type=F32)
                + jnp.dot(mid.astype(BF16), tri, preferred_element_type=F32)
                + jnp.dot(lo.astype(BF16), tri, preferred_element_type=F32))

    c_c = cumsum_lanes(lfc_ref[0], tric_ref[...])
    c_n = (cumsum_lanes(lfn_ref[0], trin_ref[...]) + c_c[:, past - 1:past]) * LOG2E
    c_c = c_c * LOG2E

    def per_row(x):
        return jnp.concatenate([jnp.broadcast_to(x[h:h + 1], (n_new, x.shape[1])) for h in range(H_FOX)], axis=0)

    cn_rows = per_row(c_n)
    colq = lax.broadcasted_iota(jnp.int32, (rows, n_new), 1)
    rowq = lax.broadcasted_iota(jnp.int32, (rows, n_new), 0) % n_new
    c_q = jnp.sum(jnp.where(colq == rowq, cn_rows, 0.0), axis=-1, keepdims=True)

    s_c = s_c + c_q - per_row(c_c)
    s_n = jnp.where(colq <= rowq, s_n + c_q - cn_rows, -jnp.inf)
    m = jnp.maximum(jnp.max(s_c, axis=-1, keepdims=True), jnp.max(s_n, axis=-1, keepdims=True))
    p_c = jnp.exp2(s_c - m)
    p_n = jnp.exp2(s_n - m)
    denom = jnp.sum(p_c, axis=-1, keepdims=True) + jnp.sum(p_n, axis=-1, keepdims=True)
    pv = (jnp.dot(p_c.astype(BF16), vc_ref[0].astype(BF16), preferred_element_type=F32)
          + jnp.dot(p_n.astype(BF16), v_new, preferred_element_type=F32)) / denom
    pv = pv.astype(BF16)
    for h in range(H_FOX):
        o_ref[h] = lax.dot_general(pv[h * n_new:(h + 1) * n_new], place_ref[h], nt,
                                   preferred_element_type=F32).astype(o_ref.dtype)


def _fox_sample(qp, k_new, v_new, k_cache, v_cache, lf_cache_t, lf_new_t, n_streams, n_new):
    past = k_cache.shape[1]
    r = np.arange(past)
    tric = jnp.asarray((r[:, None] <= r[None, :]).astype(np.float32), dtype=BF16)
    rn = np.arange(n_new)
    trin = jnp.asarray((rn[:, None] <= rn[None, :]).astype(np.float32), dtype=BF16)
    place_np = np.zeros((H_FOX, LANES, FOX_W), np.float32)
    for h in range(H_FOX):
        place_np[h, np.arange(DH_FOX), h * DH_FOX + np.arange(DH_FOX)] = 1.0
    place = jnp.asarray(place_np, dtype=BF16)
    hm = pl.BlockSpec((H_FOX, n_new, LANES), lambda s: (0, s, 0))
    cache = pl.BlockSpec((1, past, FOX_W), lambda s: (s, 0, 0))
    return pl.pallas_call(
        _fox_sample_kernel,
        grid=(n_streams,),
        in_specs=[hm, pl.BlockSpec((n_new, FOX_W), lambda s: (s, 0)), pl.BlockSpec((n_new, FOX_W), lambda s: (s, 0)),
                  cache, cache,
                  pl.BlockSpec((1, H_FOX, past), lambda s: (s, 0, 0)),
                  pl.BlockSpec((1, H_FOX, n_new), lambda s: (s, 0, 0)),
                  _const_spec((past, past)), _const_spec((n_new, n_new)), _const_spec((H_FOX, LANES, FOX_W))],
        out_specs=hm,
        out_shape=jax.ShapeDtypeStruct((H_FOX, n_streams * n_new, LANES), BF16),
        compiler_params=_cparams(("arbitrary",)),
        name="fox_sample",
    )(qp, k_new, v_new, k_cache, v_cache, lf_cache_t, lf_new_t, tric, trin, place)


def _merge_kernel(x_ref, ro_ref, rgs_ref, fo_ref, sga_ref, sgb_ref, gn_ref, wro_ref, wfo_ref, wo_ref, g2_ref,
                  x1_ref, xn2_ref):
    ro = ro_ref[...].astype(F32)
    parts = []
    for h in range(H_RET):
        sl = slice(h * DV_RET, (h + 1) * DV_RET)
        parts.append(_rms_rows(ro[:, sl], gn_ref[:, sl]))
    gated = (rgs_ref[...].astype(F32) * jnp.concatenate(parts, axis=-1)).astype(BF16)
    ret_y = jnp.dot(gated, wro_ref[...], preferred_element_type=F32)
    fox_y = jnp.dot(fo_ref[0], wfo_ref[0], preferred_element_type=F32)
    for h in range(1, H_FOX):
        fox_y = fox_y + jnp.dot(fo_ref[h], wfo_ref[h], preferred_element_type=F32)
    mixed = sga_ref[...].astype(F32) * ret_y + sgb_ref[...].astype(F32) * fox_y
    x1 = x_ref[...] + jnp.dot(mixed.astype(BF16), wo_ref[...], preferred_element_type=F32)
    x1_ref[...] = x1
    xn2_ref[...] = _rms_rows(x1, g2_ref[...]).astype(BF16)


def _merge(x2d, ret_o, rgs, fox_o, sga, sgb, gn, w_ro, w_fo_pad, w_o, g2, tm):
    n = x2d.shape[0]
    row = lambda w: pl.BlockSpec((tm, w), lambda i: (i, 0))
    return pl.pallas_call(
        _merge_kernel,
        grid=(n // tm,),
        in_specs=[row(D_MODEL), row(RET_V), row(RET_V), pl.BlockSpec((H_FOX, tm, LANES), lambda i: (0, i, 0)),
                  row(D_MODEL), row(D_MODEL), _const_spec((1, RET_V)), _const_spec(w_ro.shape),
                  _const_spec(w_fo_pad.shape), _const_spec(w_o.shape), _const_spec((1, D_MODEL))],
        out_specs=[row(D_MODEL), row(D_MODEL)],
        out_shape=[jax.ShapeDtypeStruct((n, D_MODEL), F32), jax.ShapeDtypeStruct((n, D_MODEL), BF16)],
        compiler_params=_cparams(("arbitrary",)),
        name="merge",
    )(x2d, ret_o, rgs, fox_o, sga, sgb, gn, w_ro, w_fo_pad, w_o, g2)


_CAND_ROWS = ([(0, j) for j in range(16)] + [(i, j) for i in range(1, 8) for j in range(8)]
              + [(i, 0) for i in range(8, 16)])
_CAND_VALID = np.array([(i + 1) * (j + 1) <= PEER_TOPK for i, j in _CAND_ROWS])
_CAND_FLAT = np.array([i * PEER_TOPK + j for i, j in _CAND_ROWS], np.float32)


def _extract_topk(scores, order, k, break_ties):
    cur = scores
    rank = jnp.full(scores.shape, float(k), F32)
    vals = []
    big = float(2 ** 20)
    for r in range(k):
        m = jnp.max(cur, axis=0, keepdims=True)
        sel = cur == m
        if break_ties:
            first = jnp.min(jnp.where(sel, order, big), axis=0, keepdims=True)
            sel = order == first
        rank = jnp.where(sel, float(r), rank)
        cur = jnp.where(sel, -jnp.inf, cur)
        vals.append(m)
    return vals, rank


def _peer_route_kernel(xn_ref, wq_ref, k1_ref, k2_ref, cflat_ref, cvalid_ref,
                       n1_ref, e1_ref, r2_ref, e2_ref, q_scr):
    tm = xn_ref.shape[0]
    q_scr[...] = jnp.dot(xn_ref[...], wq_ref[...], preferred_element_type=F32).astype(BF16)
    key_iota = lax.broadcasted_iota(jnp.int32, (PEER_N_KEYS, tm), 0).astype(F32)
    cflat = jnp.broadcast_to(cflat_ref[...], (len(_CAND_ROWS), tm))
    cvalid = jnp.broadcast_to(cvalid_ref[...], (len(_CAND_ROWS), tm)) > 0.5
    nt = (((1,), (1,)), ((), ()))

    def head(h, carry):
        col = pl.multiple_of(h * (2 * PEER_HALF), 2 * PEER_HALF)
        s1 = lax.dot_general(k1_ref[...], q_scr[:, pl.ds(col, PEER_HALF)], nt, preferred_element_type=F32)
        s2 = lax.dot_general(k2_ref[...], q_scr[:, pl.ds(col + PEER_HALF, PEER_HALF)], nt,
                             preferred_element_type=F32)

        def route(break_ties):
            v1, rank1 = _extract_topk(s1, key_iota, PEER_TOPK, break_ties)
            v2, rank2 = _extract_topk(s2, key_iota, PEER_TOPK, break_ties)
            v2_lo = jnp.concatenate(v2[:8], axis=0)
            v2_all = jnp.concatenate(v2, axis=0)
            cand = jnp.concatenate([v1[0] + v2_all] + [v1[i] + v2_lo for i in range(1, 8)]
                                   + [jnp.concatenate(v1[8:], axis=0) + v2[0]], axis=0)
            cand = jnp.where(cvalid, cand, -jnp.inf)
            _, crank = _extract_topk(cand, cflat, PEER_TOPK, break_ties)
            picked = crank < float(PEER_TOPK)
            top = v1[0] + v2[0]
            z = jnp.sum(jnp.where(picked, jnp.exp(cand - top), 0.0), axis=0, keepdims=True)
            cnt = jnp.where(picked, 1.0, 0.0)
            n_rows = [jnp.sum(cnt[0:16], axis=0, keepdims=True)]
            n_rows += [jnp.sum(cnt[16 + 8 * (i - 1):16 + 8 * i], axis=0, keepdims=True) for i in range(1, 8)]
            n_rows += [cnt[72 + i:73 + i] for i in range(8)]
            n1 = jnp.zeros_like(s1)
            for i in range(PEER_TOPK):
                n1 = jnp.where(rank1 == float(i), n_rows[i], n1)
            n1_ref[h] = n1
            e1_ref[h] = jnp.exp(s1 - v1[0]) / z
            r2_ref[h] = rank2.astype(BF16)
            e2_ref[h] = jnp.exp(s2 - v2[0]).astype(BF16)
            n_taken = (jnp.sum(jnp.where(rank1 < float(PEER_TOPK), 1.0, 0.0), axis=0, keepdims=True)
                       + jnp.sum(jnp.where(rank2 < float(PEER_TOPK), 1.0, 0.0), axis=0, keepdims=True)
                       + jnp.sum(cnt, axis=0, keepdims=True))
            return jnp.max(n_taken)

        most_taken = route(False)

        @pl.when(most_taken > 3.0 * PEER_TOPK)
        def _():
            route(True)

        return carry

    lax.fori_loop(0, PEER_HEADS, head, 0)


def _peer_route(xn2, w_q, k1, k2, tm):
    n = xn2.shape[0]
    ncand = len(_CAND_ROWS)
    cflat = jnp.asarray(_CAND_FLAT).reshape(ncand, 1)
    cvalid = jnp.asarray(_CAND_VALID.astype(np.float32)).reshape(ncand, 1)
    em = pl.BlockSpec((PEER_HEADS, PEER_N_KEYS, tm), lambda i: (0, 0, i))
    return pl.pallas_call(
        _peer_route_kernel,
        grid=(n // tm,),
        in_specs=[pl.BlockSpec((tm, D_MODEL), lambda i: (i, 0)), _const_spec(w_q.shape), _const_spec(k1.shape),
                  _const_spec(k2.shape), _const_spec((ncand, 1)), _const_spec((ncand, 1))],
        out_specs=[em, em, em, em],
        out_shape=[jax.ShapeDtypeStruct((PEER_HEADS, PEER_N_KEYS, n), dt) for dt in (F32, F32, BF16, BF16)],
        scratch_shapes=[pltpu.VMEM((tm, PEER_HEADS * 2 * PEER_HALF), BF16)],
        compiler_params=_cparams(("arbitrary",)),
        name="peer_route",
    )(xn2, w_q, k1, k2, cflat, cvalid)


PEER_EXPERT_SUB = 512
PEER_LOOKAHEAD = 2


def _peer_dense_kernel(xn_ref, x1_ref, u_ref, vt_ref, n1_ref, e1_ref, r2_ref, e2_ref, y_ref, acc_scr):
    e = pl.program_id(1)
    te = u_ref.shape[0]
    tm = xn_ref.shape[0]

    @pl.when(e == 0)
    def _():
        acc_scr[...] = jnp.zeros_like(acc_scr)

    sub = PEER_EXPERT_SUB
    n_sub = te // sub
    xn = xn_ref[...]
    zero = jnp.zeros((PEER_N_KEYS, tm), BF16)

    def pre_act(j):
        return lax.dot_general(u_ref[j * sub:(j + 1) * sub, :], xn, (((1,), (1,)), ((), ())),
                               preferred_element_type=F32)

    pending = [pre_act(j) for j in range(min(PEER_LOOKAHEAD, n_sub))]
    for j in range(n_sub):
        zt = pending.pop(0)
        if j + PEER_LOOKAHEAD < n_sub:
            pending.append(pre_act(j + PEER_LOOKAHEAD))
        act = (0.5 * zt * (1.0 + lax.erf(zt * (2.0 ** -0.5)))).astype(BF16)
        gates = []
        for a_loc in range(sub // PEER_N_KEYS):
            a = e * (te // PEER_N_KEYS) + j * (sub // PEER_N_KEYS) + a_loc
            g = zero
            for h in range(PEER_HEADS):
                n1 = n1_ref[h, pl.ds(a, 1), :].astype(BF16)
                e1 = e1_ref[h, pl.ds(a, 1), :].astype(BF16)
                g = g + jnp.where(r2_ref[h] < n1, e2_ref[h], zero) * e1
            gates.append(g)
        w = act * jnp.concatenate(gates, axis=0)
        acc_scr[...] += jnp.dot(vt_ref[:, j * sub:(j + 1) * sub], w, preferred_element_type=F32)

    @pl.when(e == pl.num_programs(1) - 1)
    def _():
        y_ref[...] = x1_ref[...] + acc_scr[...].T


def _peer_dense(xn2, x1, u_bf, vt_bf, n1, e1, r2, e2, tm, te):
    n = xn2.shape[0]
    n_exp = u_bf.shape[0]
    em = pl.BlockSpec((PEER_HEADS, PEER_N_KEYS, tm), lambda i, e: (0, 0, i))
    return pl.pallas_call(
        _peer_dense_kernel,
        grid=(n // tm, n_exp // te),
        in_specs=[pl.BlockSpec((tm, D_MODEL), lambda i, e: (i, 0)), pl.BlockSpec((tm, D_MODEL), lambda i, e: (i, 0)),
                  pl.BlockSpec((te, D_MODEL), lambda i, e: (e, 0)), pl.BlockSpec((D_MODEL, te), lambda i, e: (0, e)),
                  em, em, em, em],
        out_specs=pl.BlockSpec((tm, D_MODEL), lambda i, e: (i, 0)),
        out_shape=jax.ShapeDtypeStruct((n, D_MODEL), F32),
        scratch_shapes=[pltpu.VMEM((D_MODEL, tm), F32)],
        compiler_params=_cparams(("arbitrary", "arbitrary")),
        name="peer_dense",
    )(xn2, x1, u_bf, vt_bf, n1, e1, r2, e2)


def _state_to_pair(s):
    b = s.shape[0]
    s5 = s.reshape(b, N_PAIR, 2, DK_RET, DV_RET)
    z = jnp.zeros_like(s5[:, :, 0])
    top = jnp.concatenate([s5[:, :, 0], z], axis=-1)
    bot = jnp.concatenate([z, s5[:, :, 1]], axis=-1)
    return jnp.concatenate([top, bot], axis=-2)


def _pair_to_state(sp):
    b = sp.shape[0]
    a = sp[:, :, :DK_RET, :DV_RET]
    c = sp[:, :, DK_RET:, DV_RET:]
    return jnp.stack([a, c], axis=2).reshape(b, H_RET, DK_RET, DV_RET)


ROW_TILE = 512
RET_CHUNK = 256
ATTN_TILE = 512
ATTN_HEADS = 2
ATTN_KV_SUB = 256
ATTN_KV_SUBTILES = 8
PEER_ROUTE_TILE = 256
PEER_TOKEN_TILE = 512
PEER_EXPERT_TILE = 2048


def _layer(x, pos, seq, s0_pair, fox_cache, prm):
    batch = x.shape[0]
    n = batch * seq
    x2d = x.reshape(n, D_MODEL)
    tm = min(ROW_TILE, n)
    cosf, sinf = _rotary_tables(pos)
    if fox_cache is not None:
        cosf = jnp.tile(cosf, (batch, 1))
        sinf = jnp.tile(sinf, (batch, 1))
    rq, rk, rv, rgs, sga, sgb = _proj_ret(x2d, prm['g1'], prm['w_ret'], cosf, sinf, tm)
    blocks_per_seq = max(seq // tm, 1)
    fk, fv, logf_pad, qp, kp, vt = _proj_fox(x2d, prm['g1'], prm['w_fox'], prm['bf_pad'], prm['gq_pad'],
                                             prm['gk_pad'], prm['gk_dense'], prm['pmean'], _tri(tm), tm,
                                             blocks_per_seq)
    logf = logf_pad[:, :H_FOX]
    if fox_cache is None:
        ret_o, s_pair = _retention(rq, rk, rv, s0_pair, batch, seq, min(RET_CHUNK, seq))
        fox_o = _fox_attention(qp, kp, vt, batch, seq, min(ATTN_TILE, seq), min(ATTN_KV_SUB, seq),
                               ATTN_KV_SUBTILES, ATTN_HEADS)
    else:
        ret_o, s_pair = _retention(rq, rk, rv, s0_pair, batch, seq, seq)
        k_c, v_c, lf_c = fox_cache
        past = k_c.shape[1]
        fox_o = _fox_sample(qp, fk, fv, k_c.reshape(batch, past, FOX_W), v_c.reshape(batch, past, FOX_W),
                            lf_c.transpose(0, 2, 1), logf.reshape(batch, seq, H_FOX).transpose(0, 2, 1), batch, seq)
    x1, xn2 = _merge(x2d, ret_o, rgs, fox_o, sga, sgb, prm['gn'], prm['w_ro'], prm['w_fo_pad'], prm['w_o'],
                     prm['g2'], tm)
    n1, e1, r2, e2 = _peer_route(xn2, prm['w_q'], prm['k1'], prm['k2'], min(PEER_ROUTE_TILE, n))
    y = _peer_dense(xn2, x1, prm['u'], prm['vt'], n1, e1, r2, e2, min(PEER_TOKEN_TILE, n), PEER_EXPERT_TILE)
    return (y.reshape(batch, seq, D_MODEL), _pair_to_state(s_pair), fk.reshape(batch, seq, H_FOX, DH_FOX),
            fv.reshape(batch, seq, H_FOX, DH_FOX), logf.reshape(batch, seq, H_FOX))


def kernel(x_prompt, x_sample, state_ret, cache_fox_k, cache_fox_v, cache_fox_logf, norm1_g, w_in, b_f, fox_q_g,
           fox_k_g, ret_gn_g, w_ret_out, w_fox_out, w_o, norm2_g, peer_w_q, peer_sub_k1, peer_sub_k2, peer_u,
           peer_v):
    depth = w_in.shape[0]
    batch, seq, _ = x_prompt.shape
    dec_batch, dec_seq, _ = x_sample.shape
    past = cache_fox_k.shape[2]
    xp, xs = x_prompt, x_sample
    outs = [[] for _ in range(8)]
    for l in range(depth):
        w_ret, w_fox, bf_pad, gq_pad, gk_pad, gk_dense, pmean = _prep_mixer_params(w_in[l], b_f[l], fox_q_g[l],
                                                                                  fox_k_g[l])
        prm = dict(
            g1=norm1_g[l].reshape(1, D_MODEL), w_ret=w_ret, w_fox=w_fox, bf_pad=bf_pad, gq_pad=gq_pad, gk_pad=gk_pad,
            gk_dense=gk_dense, pmean=pmean, gn=ret_gn_g[l].reshape(1, RET_V), w_ro=w_ret_out[l].astype(BF16),
            w_fo_pad=jnp.pad(w_fox_out[l].reshape(H_FOX, DH_FOX, D_MODEL),
                             ((0, 0), (0, LANES - DH_FOX), (0, 0))).astype(BF16),
            w_o=w_o[l].astype(BF16), g2=norm2_g[l].reshape(1, D_MODEL), w_q=peer_w_q[l].astype(BF16),
            k1=peer_sub_k1[l].astype(BF16), k2=peer_sub_k2[l].astype(BF16), u=peer_u[l].astype(BF16),
            vt=peer_v[l].T.astype(BF16))
        s0_prompt = jnp.zeros((batch, N_PAIR, 2 * DK_RET, 2 * DV_RET), F32)
        xp, s_p, k_p, v_p, lf_p = _layer(xp, jnp.arange(seq), seq, s0_prompt, None, prm)
        xs, s_s, k_s, v_s, lf_s = _layer(xs, past + jnp.arange(dec_seq), dec_seq, _state_to_pair(state_ret[l]),
                                         (cache_fox_k[l], cache_fox_v[l], cache_fox_logf[l]), prm)
        for lst, val in zip(outs, (s_p, k_p, v_p, lf_p, s_s, k_s, v_s, lf_s)):
            lst.append(val)
    stacked = [jnp.stack(lst, axis=0) for lst in outs]
    return (xp, xs, *stacked)
```

```python
import functools
import math

import jax
import jax.numpy as jnp
import numpy as np
from jax import lax
from jax.experimental import pallas as pl
from jax.experimental.pallas import tpu as pltpu

D_MODEL = 1024
H_RET = 8
DK_RET = 64
DV_RET = 128
H_FOX = 8
DH_FOX = 64
ROPE_BASE = 10000.0
PEER_HEADS = 8
PEER_N_KEYS = 128
PEER_HALF = 128
PEER_TOPK = 16
EPS = 1e-6

RET_QK = H_RET * DK_RET
RET_V = H_RET * DV_RET
FOX_W = H_FOX * DH_FOX
N_PAIR = H_RET // 2
LANES = 128
VMEM_LIMIT = 56 * 1024 * 1024

BF16 = jnp.bfloat16
F32 = jnp.float32
LOG2E = math.log2(math.e)


def _cparams(sem):
    return pltpu.CompilerParams(dimension_semantics=sem, vmem_limit_bytes=VMEM_LIMIT)


def _const_spec(shape):
    nd = len(shape)
    return pl.BlockSpec(shape, lambda *_: (0,) * nd, pipeline_mode=pl.Buffered(1))


def _rms_rows(x, g):
    ms = jnp.mean(x * x, axis=-1, keepdims=True)
    return x * lax.rsqrt(ms + EPS) * g


def _split3(c):
    hi = c.astype(BF16).astype(F32)
    r = c - hi
    mid = r.astype(BF16).astype(F32)
    lo = (r - mid).astype(BF16).astype(F32)
    return hi, mid, lo


def _proj_ret_kernel(x_ref, g_ref, w_ref, cos_ref, sin_ref,
                     rq_ref, rk_ref, rv_ref, rgs_ref, sga_ref, sgb_ref):
    xn = _rms_rows(x_ref[...], g_ref[...]).astype(BF16)
    cosf = cos_ref[...]
    sinf = sin_ref[...]
    tm = xn.shape[0]
    lane = lax.broadcasted_iota(jnp.int32, (tm, RET_QK), 1)
    first_half = (lane % DK_RET) < (DK_RET // 2)

    def rotary(h):
        sw = jnp.where(first_half, pltpu.roll(h, RET_QK - DK_RET // 2, 1), pltpu.roll(h, DK_RET // 2, 1))
        parts = []
        for gidx in range(RET_QK // LANES):
            sl = slice(gidx * LANES, (gidx + 1) * LANES)
            parts.append(h[:, sl] * cosf + sw[:, sl] * sinf)
        return jnp.concatenate(parts, axis=-1)

    def proj(lo, width):
        return jnp.dot(xn, w_ref[:, lo:lo + width], preferred_element_type=F32)

    rq_ref[...] = rotary(proj(0, RET_QK)).astype(BF16)
    rk_ref[...] = (rotary(proj(RET_QK, RET_QK)) * (DK_RET ** -0.5)).astype(BF16)
    off = 2 * RET_QK
    rv_ref[...] = proj(off, RET_V).astype(BF16)
    rg = proj(off + RET_V, RET_V)
    rgs_ref[...] = (rg * jax.nn.sigmoid(rg)).astype(BF16)
    sga_ref[...] = jax.nn.sigmoid(proj(off + 2 * RET_V, D_MODEL)).astype(BF16)
    sgb_ref[...] = jax.nn.sigmoid(proj(off + 2 * RET_V + D_MODEL, D_MODEL)).astype(BF16)


def _proj_ret(x2d, g1, w_ret, cosf, sinf, tm):
    n = x2d.shape[0]
    nblk = n // tm
    tab_blocks = cosf.shape[0] // tm
    row = lambda w: pl.BlockSpec((tm, w), lambda i: (i, 0))
    tab = pl.BlockSpec((tm, LANES), lambda i: (i % tab_blocks, 0))
    widths = (RET_QK, RET_QK, RET_V, RET_V, D_MODEL, D_MODEL)
    return pl.pallas_call(
        _proj_ret_kernel,
        grid=(nblk,),
        in_specs=[row(D_MODEL), _const_spec((1, D_MODEL)), _const_spec(w_ret.shape), tab, tab],
        out_specs=[row(w) for w in widths],
        out_shape=[jax.ShapeDtypeStruct((n, w), BF16) for w in widths],
        compiler_params=_cparams(("arbitrary",)),
        name="proj_ret",
    )(x2d, g1, w_ret, cosf, sinf)


def _proj_fox_kernel(blocks_per_seq, x_ref, g_ref, w_ref, bf_ref, gq_ref, gkp_ref, gkd_ref, pmean_ref, tri_ref,
                     fk_ref, fv_ref, logf_ref, qp_ref, kp_ref, vt_ref, cend_ref, carry_ref):
    i = pl.program_id(0)
    xn = _rms_rows(x_ref[...], g_ref[...]).astype(BF16)
    tm = xn.shape[0]

    def proj(lo, width):
        return jnp.dot(xn, w_ref[:, lo:lo + width], preferred_element_type=F32)

    hk = proj(0, FOX_W)
    sq = hk * hk
    sq_hi = sq.astype(BF16)
    sq_lo = (sq - sq_hi.astype(F32)).astype(BF16)
    ms = (jnp.dot(sq_hi, pmean_ref[...], preferred_element_type=F32)
          + jnp.dot(sq_lo, pmean_ref[...], preferred_element_type=F32))
    fk_ref[...] = hk * lax.rsqrt(ms + EPS) * gkd_ref[...]
    fv_ref[...] = proj(FOX_W, FOX_W)

    z = proj(2 * FOX_W, LANES) + bf_ref[...]
    logf = jnp.minimum(z, 0.0) - jnp.log1p(jnp.exp(-jnp.abs(z)))
    logf_ref[...] = logf

    @pl.when(i % blocks_per_seq == 0)
    def _():
        carry_ref[...] = jnp.zeros_like(carry_ref)

    l_hi, l_mid, l_lo = _split3(logf)
    tri = tri_ref[...]
    c = (jnp.dot(tri, l_hi.astype(BF16), preferred_element_type=F32)
         + jnp.dot(tri, l_mid.astype(BF16), preferred_element_type=F32)
         + jnp.dot(tri, l_lo.astype(BF16), preferred_element_type=F32)) + carry_ref[...]
    carry_ref[...] = c[tm - 1:tm, :]
    cend_ref[0] = jnp.broadcast_to(c[tm - 1:tm, :], cend_ref.shape[1:])

    lane = lax.broadcasted_iota(jnp.int32, (tm, LANES), 1)
    is_vec = lane < DH_FOX
    base = 2 * FOX_W + LANES
    for h in range(H_FOX):
        ch = jnp.broadcast_to(c[:, h:h + 1], (tm, LANES)) * LOG2E
        c_hi, c_mid, c_lo = _split3(ch)
        one = jnp.ones_like(ch)
        zero = jnp.zeros_like(ch)
        aug_q = jnp.where(lane == 64, c_hi, jnp.where(lane == 65, c_mid, jnp.where(
            lane == 66, c_lo, jnp.where(lane < 70, one, zero))))
        aug_k = jnp.where(lane < 67, one, jnp.where(lane == 67, -c_hi, jnp.where(
            lane == 68, -c_mid, jnp.where(lane == 69, -c_lo, zero))))
        hq = proj(base + h * LANES, LANES)
        hq = hq * lax.rsqrt(jnp.sum(hq * hq, axis=-1, keepdims=True) * (1.0 / DH_FOX) + EPS) * gq_ref[...]
        qp_ref[h] = jnp.where(is_vec, hq * (DH_FOX ** -0.5 * LOG2E), aug_q).astype(BF16)
        hkp = proj(base + (H_FOX + h) * LANES, LANES)
        hkp = hkp * lax.rsqrt(jnp.sum(hkp * hkp, axis=-1, keepdims=True) * (1.0 / DH_FOX) + EPS) * gkp_ref[...]
        kp_ref[h] = jnp.where(is_vec, hkp, aug_k).astype(BF16)
        hv = proj(base + (2 * H_FOX + h) * LANES, LANES)
        vt_ref[h] = jnp.where(is_vec, hv, one).T[:vt_ref.shape[1]].astype(BF16)


def _proj_fox(x2d, g1, w_fox, bf_pad, gq_pad, gk_pad, gk_dense, pmean, tri, tm, blocks_per_seq):
    n = x2d.shape[0]
    nblk = n // tm
    row = lambda w: pl.BlockSpec((tm, w), lambda i: (i, 0))
    hm = pl.BlockSpec((H_FOX, tm, LANES), lambda i: (0, i, 0))
    return pl.pallas_call(
        functools.partial(_proj_fox_kernel, blocks_per_seq),
        grid=(nblk,),
        in_specs=[row(D_MODEL), _const_spec((1, D_MODEL)), _const_spec(w_fox.shape), _const_spec((1, LANES)),
                  _const_spec((1, LANES)), _const_spec((1, LANES)), _const_spec((1, FOX_W)),
                  _const_spec((FOX_W, FOX_W)), _const_spec((tm, tm))],
        out_specs=[row(FOX_W), row(FOX_W), row(LANES), hm, hm,
                   pl.BlockSpec((H_FOX, V_ROWS, tm), lambda i: (0, 0, i)),
                   pl.BlockSpec((1, 8, LANES), lambda i: (i, 0, 0))],
        out_shape=[jax.ShapeDtypeStruct((n, FOX_W), F32), jax.ShapeDtypeStruct((n, FOX_W), F32),
                   jax.ShapeDtypeStruct((n, LANES), F32), jax.ShapeDtypeStruct((H_FOX, n, LANES), BF16),
                   jax.ShapeDtypeStruct((H_FOX, n, LANES), BF16), jax.ShapeDtypeStruct((H_FOX, V_ROWS, n), BF16),
                   jax.ShapeDtypeStruct((nblk, 8, LANES), F32)],
        scratch_shapes=[pltpu.VMEM((1, LANES), F32)],
        compiler_params=_cparams(("arbitrary",)),
        name="proj_fox",
    )(x2d, g1, w_fox, bf_pad, gq_pad, gk_pad, gk_dense, pmean, tri)


def _rotary_tables(pos):
    half = DK_RET // 2
    inv = ROPE_BASE ** (-jnp.arange(half, dtype=F32) / half)
    ang = pos.astype(F32)[:, None] * inv[None, :]
    cos = jnp.cos(ang)
    sin = jnp.sin(ang)
    cosf = jnp.concatenate([cos, cos, cos, cos], axis=-1)
    sinf = jnp.concatenate([-sin, sin, -sin, sin], axis=-1)
    return cosf, sinf


def _pad_heads(w, n_heads, dh):
    d = w.shape[0]
    w3 = w.reshape(d, n_heads, dh)
    return jnp.pad(w3, ((0, 0), (0, 0), (0, LANES - dh))).reshape(d, n_heads * LANES)


def _prep_mixer_params(w_in, b_f, fox_q_g, fox_k_g):
    sizes = (RET_QK, RET_QK, RET_V, RET_V, FOX_W, FOX_W, FOX_W, H_FOX, D_MODEL, D_MODEL)
    offs = np.concatenate([[0], np.cumsum(sizes)])
    col = lambda j: w_in[:, offs[j]:offs[j + 1]]
    w_ret = jnp.concatenate([col(0), col(1), col(2), col(3), col(8), col(9)], axis=1).astype(BF16)
    w_fz = jnp.pad(col(7), ((0, 0), (0, LANES - H_FOX)))
    w_fox = jnp.concatenate([col(5), col(6), w_fz, _pad_heads(col(4), H_FOX, DH_FOX),
                             _pad_heads(col(5), H_FOX, DH_FOX), _pad_heads(col(6), H_FOX, DH_FOX)],
                            axis=1).astype(BF16)
    bf_pad = jnp.pad(b_f, (0, LANES - H_FOX)).reshape(1, LANES)
    gq_pad = jnp.pad(fox_q_g, (0, LANES - DH_FOX)).reshape(1, LANES)
    gk_pad = jnp.pad(fox_k_g, (0, LANES - DH_FOX)).reshape(1, LANES)
    gk_dense = jnp.tile(fox_k_g, H_FOX).reshape(1, FOX_W)
    grp = np.arange(FOX_W) // DH_FOX
    pmean = jnp.asarray((grp[:, None] == grp[None, :]).astype(np.float32) / DH_FOX, dtype=BF16)
    return w_ret, w_fox, bf_pad, gq_pad, gk_pad, gk_dense, pmean


def _tri(tm):
    r = np.arange(tm)
    return jnp.asarray((r[None, :] <= r[:, None]).astype(np.float32), dtype=BF16)


def _retention_tables(chunk):
    log_gamma = jnp.log1p(-jnp.exp2(-5.0 - jnp.arange(H_RET, dtype=F32)))
    idx = jnp.arange(chunk, dtype=F32)
    diff = idx[:, None] - idx[None, :]
    dmat = jnp.where(diff >= 0, jnp.exp(jnp.maximum(diff, 0.0)[None] * log_gamma[:, None, None]), 0.0)
    lg_qk = jnp.repeat(log_gamma, DK_RET).reshape(N_PAIR, 1, 2 * DK_RET)
    qdec = jnp.exp((idx + 1.0)[None, :, None] * lg_qk)
    kdec = jnp.exp((chunk - 1.0 - idx)[None, :, None] * lg_qk)
    sdec = jnp.broadcast_to(jnp.exp(chunk * lg_qk).reshape(N_PAIR, 2 * DK_RET, 1), (N_PAIR, 2 * DK_RET, 2 * DV_RET))
    row_head = np.arange(2 * DK_RET) // DK_RET
    col_head = np.arange(2 * DV_RET) // DV_RET
    bmask = jnp.asarray((row_head[:, None] == col_head[None, :]).astype(np.float32))
    return dmat, qdec, kdec, sdec, bmask


def _retention_kernel(q_ref, k_ref, v_ref, s0_ref, dmat_ref, qdec_ref, kdec_ref, sdec_ref, bmask_ref,
                      o_ref, sout_ref, s_scr):
    c = pl.program_id(2)

    @pl.when(c == 0)
    def _():
        s_scr[...] = s0_ref[0, 0]

    q = q_ref[...]
    k = k_ref[...]
    v = v_ref[...]
    lane = lax.broadcasted_iota(jnp.int32, q.shape, 1)
    zero = jnp.zeros_like(q)
    nt = (((1,), (1,)), ((), ()))
    halves = []
    for j in range(2):
        qj = jnp.where((lane < DK_RET) == (j == 0), q, zero)
        sc = lax.dot_general(qj, k, nt, preferred_element_type=F32) * dmat_ref[j]
        halves.append(jnp.dot(sc.astype(BF16), v[:, j * DV_RET:(j + 1) * DV_RET], preferred_element_type=F32))
    inner = jnp.concatenate(halves, axis=-1)
    s = s_scr[...]
    qd = (q.astype(F32) * qdec_ref[0]).astype(BF16)
    cross = jnp.dot(qd, s.astype(BF16), preferred_element_type=F32)
    o_ref[...] = (inner + cross).astype(o_ref.dtype)
    kd = (k.astype(F32) * kdec_ref[0]).astype(BF16)
    upd = lax.dot_general(kd, v, (((0,), (0,)), ((), ())), preferred_element_type=F32)
    s_new = sdec_ref[0] * s + bmask_ref[...] * upd
    s_scr[...] = s_new

    @pl.when(c == pl.num_programs(2) - 1)
    def _():
        sout_ref[0, 0] = s_new


def _retention(rq, rk, rv, s0_pair, batch, seq, chunk):
    n = rq.shape[0]
    nchunk = seq // chunk
    dmat, qdec, kdec, sdec, bmask = _retention_tables(chunk)
    qk_spec = pl.BlockSpec((chunk, 2 * DK_RET), lambda b, p, c: (b * nchunk + c, p))
    v_spec = pl.BlockSpec((chunk, 2 * DV_RET), lambda b, p, c: (b * nchunk + c, p))
    s_spec = pl.BlockSpec((1, 1, 2 * DK_RET, 2 * DV_RET), lambda b, p, c: (b, p, 0, 0))
    pair_tab = lambda shape: pl.BlockSpec((1,) + shape, lambda b, p, c: (p, 0, 0))
    return pl.pallas_call(
        _retention_kernel,
        grid=(batch, N_PAIR, nchunk),
        in_specs=[qk_spec, qk_spec, v_spec, s_spec,
                  pl.BlockSpec((2, chunk, chunk), lambda b, p, c: (p, 0, 0)),
                  pair_tab((chunk, 2 * DK_RET)), pair_tab((chunk, 2 * DK_RET)),
                  pair_tab((2 * DK_RET, 2 * DV_RET)),
                  pl.BlockSpec((2 * DK_RET, 2 * DV_RET), lambda b, p, c: (0, 0))],
        out_specs=[v_spec, s_spec],
        out_shape=[jax.ShapeDtypeStruct((n, RET_V), BF16),
                   jax.ShapeDtypeStruct((batch, N_PAIR, 2 * DK_RET, 2 * DV_RET), F32)],
        scratch_shapes=[pltpu.VMEM((2 * DK_RET, 2 * DV_RET), F32)],
        compiler_params=_cparams(("arbitrary", "arbitrary", "arbitrary")),
        name="retention",
    )(rq, rk, rv, s0_pair, dmat, qdec, kdec, sdec, bmask)


NEG_BIG = -1e30
ATTN_LOOKAHEAD = 2


V_ROWS = DH_FOX + 16


def _fox_attn_kernel(n_sub, nq, qi_ref, kj_ref, ks_ref, q_ref, k_ref, vt_ref, o_ref, m_scr, acc_scr):
    s_idx = pl.program_id(2)
    qi = qi_ref[s_idx]
    kj = kj_ref[s_idx]
    kstart = ks_ref[(pl.program_id(0) * pl.num_programs(1) + pl.program_id(1)) * nq + qi]
    n_heads = q_ref.shape[0]
    tq = q_ref.shape[1]
    sub = k_ref.shape[1] // n_sub
    last_kj = (qi * tq) // (n_sub * sub)

    @pl.when(kj == kstart)
    def _():
        m_scr[...] = jnp.full_like(m_scr, NEG_BIG)
        acc_scr[...] = jnp.zeros_like(acc_scr)

    def sweep(masked):
        qs = [q_ref[g] for g in range(n_heads)]
        ms = [m_scr[g] for g in range(n_heads)]
        accs = [acc_scr[g] for g in range(n_heads)]

        def scores(j):
            return [lax.dot_general(k_ref[g, j * sub:(j + 1) * sub, :], qs[g], (((1,), (1,)), ((), ())),
                                    preferred_element_type=F32) for g in range(n_heads)]

        pending = [scores(j) for j in range(min(ATTN_LOOKAHEAD, n_sub))]
        for j in range(n_sub):
            sts = pending.pop(0)
            if j + ATTN_LOOKAHEAD < n_sub:
                pending.append(scores(j + ATTN_LOOKAHEAD))
            for g in range(n_heads):
                st = sts[g]
                if masked:
                    kv_pos = kj * (n_sub * sub) + j * sub + lax.broadcasted_iota(jnp.int32, st.shape, 0)
                    q_pos = qi * tq + lax.broadcasted_iota(jnp.int32, st.shape, 1)
                    st = jnp.where(kv_pos <= q_pos, st, -jnp.inf)
                m_new = jnp.maximum(ms[g], jnp.max(st, axis=0, keepdims=True))
                alpha = jnp.exp2(ms[g] - m_new)
                p = jnp.exp2(st - m_new).astype(BF16)
                accs[g] = alpha * accs[g] + jnp.dot(vt_ref[g, :, j * sub:(j + 1) * sub], p,
                                                    preferred_element_type=F32)
                ms[g] = m_new
        for g in range(n_heads):
            m_scr[g] = ms[g]
            acc_scr[g] = accs[g]

    @pl.when((kj >= kstart) & (kj < last_kj))
    def _():
        sweep(False)

    @pl.when(kj == last_kj)
    def _():
        sweep(True)
        for g in range(n_heads):
            acc = acc_scr[g]
            o = acc / acc[DH_FOX:DH_FOX + 1, :]
            o = jnp.concatenate([o, jnp.zeros((LANES - V_ROWS, tq), F32)], axis=0)
            o_ref[g] = o.T.astype(o_ref.dtype)


UNDERFLOW_LOG2 = 160.0


def _attn_first_block(cend, g_q, g_k, batch, seq, tq, tkv, n_heads):
    nq = seq // tq
    r = tkv // tq
    nkv = seq // tkv
    qk = 1.02 * (jnp.max(jnp.abs(g_q)) * LOG2E) * (DH_FOX ** 0.5 * jnp.max(jnp.abs(g_k)))
    e = cend[:, 0, :H_FOX].reshape(batch, nq, H_FOX)
    c_q = jnp.concatenate([jnp.zeros((batch, 1, H_FOX), F32), e[:, :-1]], axis=1)
    c_k = e[:, r - 1::r]
    gap = (c_q[:, :, None, :] - c_k[:, None, :, :]) * LOG2E
    skippable = (2.0 * qk + gap) < -UNDERFLOW_LOG2
    last_kj = (jnp.arange(nq) * tq) // tkv
    before_diag = jnp.arange(nkv)[None, :] < last_kj[:, None]
    kstart = jnp.sum(jnp.where(skippable & before_diag[None, :, :, None], 1, 0), axis=2)
    kstart = jnp.min(kstart.reshape(batch, nq, H_FOX // n_heads, n_heads), axis=-1)
    return kstart.transpose(2, 0, 1).reshape(-1).astype(jnp.int32)


def _fox_attention(qp, kp, vt, cend, g_q, g_k, batch, seq, tq, sub, n_sub, n_heads):
    n = qp.shape[1]
    nq = seq // tq
    tkv = min(n_sub * sub, seq)
    n_sub = tkv // sub
    nkv = seq // tkv
    steps = [(i, j) for i in range(nq) for j in range((i * tq) // tkv + 1)]
    qi_tab = np.array([s[0] for s in steps], np.int32)
    kj_tab = np.array([s[1] for s in steps], np.int32)
    kstart = _attn_first_block(cend, g_q, g_k, batch, seq, tq, tkv, n_heads)

    def kv_block(h, b, s, qi, kj, ks):
        return b * nkv + jnp.maximum(kj[s], ks[(h * batch + b) * nq + qi[s]])

    q_spec = pl.BlockSpec((n_heads, tq, LANES), lambda h, b, s, qi, kj, ks: (h, b * nq + qi[s], 0))
    k_spec = pl.BlockSpec((n_heads, tkv, LANES), lambda h, b, s, qi, kj, ks: (h, kv_block(h, b, s, qi, kj, ks), 0))
    vt_spec = pl.BlockSpec((n_heads, V_ROWS, tkv), lambda h, b, s, qi, kj, ks: (h, 0, kv_block(h, b, s, qi, kj, ks)))
    return pl.pallas_call(
        functools.partial(_fox_attn_kernel, n_sub, nq),
        grid_spec=pltpu.PrefetchScalarGridSpec(
            num_scalar_prefetch=3,
            grid=(H_FOX // n_heads, batch, len(steps)),
            in_specs=[q_spec, k_spec, vt_spec],
            out_specs=q_spec,
            scratch_shapes=[pltpu.VMEM((n_heads, 1, tq), F32), pltpu.VMEM((n_heads, V_ROWS, tq), F32)],
        ),
        out_shape=jax.ShapeDtypeStruct((H_FOX, n, LANES), BF16),
        compiler_params=_cparams(("arbitrary", "arbitrary", "arbitrary")),
        name="fox_attention",
    )(jnp.asarray(qi_tab), jnp.asarray(kj_tab), kstart, qp, kp, vt)


def _fox_sample_kernel(qp_ref, kn_ref, vn_ref, kc_ref, vc_ref, lfc_ref, lfn_ref, tric_ref, trin_ref, place_ref,
                       o_ref):
    n_new = qp_ref.shape[1]
    past = kc_ref.shape[1]
    rows = H_FOX * n_new
    q_wide = jnp.concatenate([jnp.dot(qp_ref[h], place_ref[h], preferred_element_type=F32)
                              for h in range(H_FOX)], axis=0).astype(BF16)
    k_new = kn_ref[...].astype(BF16)
    v_new = vn_ref[...].astype(BF16)

    nt = (((1,), (1,)), ((), ()))
    s_c = lax.dot_general(q_wide, kc_ref[0].astype(BF16), nt, preferred_element_type=F32)
    s_n = lax.dot_general(q_wide, k_new, nt, preferred_element_type=F32)

    def cumsum_lanes(x, tri):
        hi, mid, lo = _split3(x)
        return (jnp.dot(hi.astype(BF16), tri, preferred_element_type=F32)
                + jnp.dot(mid.astype(BF16), tri, preferred_element_type=F32)
                + jnp.dot(lo.astype(BF16), tri, preferred_element_type=F32))

    c_c = cumsum_lanes(lfc_ref[0], tric_ref[...])
    c_n = (cumsum_lanes(lfn_ref[0], trin_ref[...]) + c_c[:, past - 1:past]) * LOG2E
    c_c = c_c * LOG2E

    def per_row(x):
        return jnp.concatenate([jnp.broadcast_to(x[h:h + 1], (n_new, x.shape[1])) for h in range(H_FOX)], axis=0)

    cn_rows = per_row(c_n)
    colq = lax.broadcasted_iota(jnp.int32, (rows, n_new), 1)
    rowq = lax.broadcasted_iota(jnp.int32, (rows, n_new), 0) % n_new
    c_q = jnp.sum(jnp.where(colq == rowq, cn_rows, 0.0), axis=-1, keepdims=True)

    s_c = s_c + c_q - per_row(c_c)
    s_n = jnp.where(colq <= rowq, s_n + c_q - cn_rows, -jnp.inf)
    m = jnp.maximum(jnp.max(s_c, axis=-1, keepdims=True), jnp.max(s_n, axis=-1, keepdims=True))
    p_c = jnp.exp2(s_c - m)
    p_n = jnp.exp2(s_n - m)
    denom = jnp.sum(p_c, axis=-1, keepdims=True) + jnp.sum(p_n, axis=-1, keepdims=True)
    pv = (jnp.dot(p_c.astype(BF16), vc_ref[0].astype(BF16), preferred_element_type=F32)
          + jnp.dot(p_n.astype(BF16), v_new, preferred_element_type=F32)) / denom
    pv = pv.astype(BF16)
    for h in range(H_FOX):
        o_ref[h] = lax.dot_general(pv[h * n_new:(h + 1) * n_new], place_ref[h], nt,
                                   preferred_element_type=F32).astype(o_ref.dtype)


def _fox_sample(qp, k_new, v_new, k_cache, v_cache, lf_cache_t, lf_new_t, n_streams, n_new):
    past = k_cache.shape[1]
    r = np.arange(past)
    tric = jnp.asarray((r[:, None] <= r[None, :]).astype(np.float32), dtype=BF16)
    rn = np.arange(n_new)
    trin = jnp.asarray((rn[:, None] <= rn[None, :]).astype(np.float32), dtype=BF16)
    place_np = np.zeros((H_FOX, LANES, FOX_W), np.float32)
    for h in range(H_FOX):
        place_np[h, np.arange(DH_FOX), h * DH_FOX + np.arange(DH_FOX)] = 1.0
    place = jnp.asarray(place_np, dtype=BF16)
    hm = pl.BlockSpec((H_FOX, n_new, LANES), lambda s: (0, s, 0))
    cache = pl.BlockSpec((1, past, FOX_W), lambda s: (s, 0, 0))
    return pl.pallas_call(
        _fox_sample_kernel,
        grid=(n_streams,),
        in_specs=[hm, pl.BlockSpec((n_new, FOX_W), lambda s: (s, 0)), pl.BlockSpec((n_new, FOX_W), lambda s: (s, 0)),
                  cache, cache,
                  pl.BlockSpec((1, H_FOX, past), lambda s: (s, 0, 0)),
                  pl.BlockSpec((1, H_FOX, n_new), lambda s: (s, 0, 0)),
                  _const_spec((past, past)), _const_spec((n_new, n_new)), _const_spec((H_FOX, LANES, FOX_W))],
        out_specs=hm,
        out_shape=jax.ShapeDtypeStruct((H_FOX, n_streams * n_new, LANES), BF16),
        compiler_params=_cparams(("arbitrary",)),
        name="fox_sample",
    )(qp, k_new, v_new, k_cache, v_cache, lf_cache_t, lf_new_t, tric, trin, place)


def _merge_kernel(x_ref, ro_ref, rgs_ref, fo_ref, sga_ref, sgb_ref, gn_ref, wro_ref, wfo_ref, wo_ref, g2_ref,
                  x1_ref, xn2_ref):
    ro = ro_ref[...].astype(F32)
    parts = []
    for h in range(H_RET):
        sl = slice(h * DV_RET, (h + 1) * DV_RET)
        parts.append(_rms_rows(ro[:, sl], gn_ref[:, sl]))
    gated = (rgs_ref[...].astype(F32) * jnp.concatenate(parts, axis=-1)).astype(BF16)
    ret_y = jnp.dot(gated, wro_ref[...], preferred_element_type=F32)
    fox_y = jnp.dot(fo_ref[0], wfo_ref[0], preferred_element_type=F32)
    for h in range(1, H_FOX):
        fox_y = fox_y + jnp.dot(fo_ref[h], wfo_ref[h], preferred_element_type=F32)
    mixed = sga_ref[...].astype(F32) * ret_y + sgb_ref[...].astype(F32) * fox_y
    x1 = x_ref[...] + jnp.dot(mixed.astype(BF16), wo_ref[...], preferred_element_type=F32)
    x1_ref[...] = x1
    xn2_ref[...] = _rms_rows(x1, g2_ref[...]).astype(BF16)


def _merge(x2d, ret_o, rgs, fox_o, sga, sgb, gn, w_ro, w_fo_pad, w_o, g2, tm):
    n = x2d.shape[0]
    row = lambda w: pl.BlockSpec((tm, w), lambda i: (i, 0))
    return pl.pallas_call(
        _merge_kernel,
        grid=(n // tm,),
        in_specs=[row(D_MODEL), row(RET_V), row(RET_V), pl.BlockSpec((H_FOX, tm, LANES), lambda i: (0, i, 0)),
                  row(D_MODEL), row(D_MODEL), _const_spec((1, RET_V)), _const_spec(w_ro.shape),
                  _const_spec(w_fo_pad.shape), _const_spec(w_o.shape), _const_spec((1, D_MODEL))],
        out_specs=[row(D_MODEL), row(D_MODEL)],
        out_shape=[jax.ShapeDtypeStruct((n, D_MODEL), F32), jax.ShapeDtypeStruct((n, D_MODEL), BF16)],
        compiler_params=_cparams(("arbitrary",)),
        name="merge",
    )(x2d, ret_o, rgs, fox_o, sga, sgb, gn, w_ro, w_fo_pad, w_o, g2)


_CAND_ROWS = ([(0, j) for j in range(16)] + [(i, j) for i in range(1, 8) for j in range(8)]
              + [(i, 0) for i in range(8, 16)])
_CAND_VALID = np.array([(i + 1) * (j + 1) <= PEER_TOPK for i, j in _CAND_ROWS])
_CAND_FLAT = np.array([i * PEER_TOPK + j for i, j in _CAND_ROWS], np.float32)


def _extract_topk(scores, order, k, break_ties):
    cur = scores
    rank = jnp.full(scores.shape, float(k), F32)
    vals = []
    big = float(2 ** 20)
    for r in range(k):
        m = jnp.max(cur, axis=0, keepdims=True)
        sel = cur == m
        if break_ties:
            first = jnp.min(jnp.where(sel, order, big), axis=0, keepdims=True)
            sel = order == first
        rank = jnp.where(sel, float(r), rank)
        cur = jnp.where(sel, -jnp.inf, cur)
        vals.append(m)
    return vals, rank


def _peer_route_kernel(xn_ref, wq_ref, k1_ref, k2_ref, cflat_ref, cvalid_ref,
                       n1_ref, e1_ref, r2_ref, e2_ref, q_scr):
    tm = xn_ref.shape[0]
    q_scr[...] = jnp.dot(xn_ref[...], wq_ref[...], preferred_element_type=F32).astype(BF16)
    key_iota = lax.broadcasted_iota(jnp.int32, (PEER_N_KEYS, tm), 0).astype(F32)
    cflat = jnp.broadcast_to(cflat_ref[...], (len(_CAND_ROWS), tm))
    cvalid = jnp.broadcast_to(cvalid_ref[...], (len(_CAND_ROWS), tm)) > 0.5
    nt = (((1,), (1,)), ((), ()))

    def head(h, carry):
        col = pl.multiple_of(h * (2 * PEER_HALF), 2 * PEER_HALF)
        s1 = lax.dot_general(k1_ref[...], q_scr[:, pl.ds(col, PEER_HALF)], nt, preferred_element_type=F32)
        s2 = lax.dot_general(k2_ref[...], q_scr[:, pl.ds(col + PEER_HALF, PEER_HALF)], nt,
                             preferred_element_type=F32)

        def route(break_ties):
            v1, rank1 = _extract_topk(s1, key_iota, PEER_TOPK, break_ties)
            v2, rank2 = _extract_topk(s2, key_iota, PEER_TOPK, break_ties)
            v2_lo = jnp.concatenate(v2[:8], axis=0)
            v2_all = jnp.concatenate(v2, axis=0)
            cand = jnp.concatenate([v1[0] + v2_all] + [v1[i] + v2_lo for i in range(1, 8)]
                                   + [jnp.concatenate(v1[8:], axis=0) + v2[0]], axis=0)
            cand = jnp.where(cvalid, cand, -jnp.inf)
            _, crank = _extract_topk(cand, cflat, PEER_TOPK, break_ties)
            picked = crank < float(PEER_TOPK)
            top = v1[0] + v2[0]
            z = jnp.sum(jnp.where(picked, jnp.exp(cand - top), 0.0), axis=0, keepdims=True)
            cnt = jnp.where(picked, 1.0, 0.0)
            n_rows = [jnp.sum(cnt[0:16], axis=0, keepdims=True)]
            n_rows += [jnp.sum(cnt[16 + 8 * (i - 1):16 + 8 * i], axis=0, keepdims=True) for i in range(1, 8)]
            n_rows += [cnt[72 + i:73 + i] for i in range(8)]
            n1 = jnp.zeros_like(s1)
            for i in range(PEER_TOPK):
                n1 = jnp.where(rank1 == float(i), n_rows[i], n1)
            n1_ref[h] = n1
            e1_ref[h] = jnp.exp(s1 - v1[0]) / z
            r2_ref[h] = rank2.astype(BF16)
            e2_ref[h] = jnp.exp(s2 - v2[0]).astype(BF16)
            n_taken = (jnp.sum(jnp.where(rank1 < float(PEER_TOPK), 1.0, 0.0), axis=0, keepdims=True)
                       + jnp.sum(jnp.where(rank2 < float(PEER_TOPK), 1.0, 0.0), axis=0, keepdims=True)
                       + jnp.sum(cnt, axis=0, keepdims=True))
            return jnp.max(n_taken)

        most_taken = route(False)

        @pl.when(most_taken > 3.0 * PEER_TOPK)
        def _():
            route(True)

        return carry

    lax.fori_loop(0, PEER_HEADS, head, 0)


def _peer_route(xn2, w_q, k1, k2, tm):
    n = xn2.shape[0]
    ncand = len(_CAND_ROWS)
    cflat = jnp.asarray(_CAND_FLAT).reshape(ncand, 1)
    cvalid = jnp.asarray(_CAND_VALID.astype(np.float32)).reshape(ncand, 1)
    em = pl.BlockSpec((PEER_HEADS, PEER_N_KEYS, tm), lambda i: (0, 0, i))
    return pl.pallas_call(
        _peer_route_kernel,
        grid=(n // tm,),
        in_specs=[pl.BlockSpec((tm, D_MODEL), lambda i: (i, 0)), _const_spec(w_q.shape), _const_spec(k1.shape),
                  _const_spec(k2.shape), _const_spec((ncand, 1)), _const_spec((ncand, 1))],
        out_specs=[em, em, em, em],
        out_shape=[jax.ShapeDtypeStruct((PEER_HEADS, PEER_N_KEYS, n), dt) for dt in (F32, F32, BF16, BF16)],
        scratch_shapes=[pltpu.VMEM((tm, PEER_HEADS * 2 * PEER_HALF), BF16)],
        compiler_params=_cparams(("arbitrary",)),
        name="peer_route",
    )(xn2, w_q, k1, k2, cflat, cvalid)


PEER_EXPERT_SUB = 512
PEER_LOOKAHEAD = 2


def _peer_dense_kernel(xn_ref, x1_ref, u_ref, vt_ref, n1_ref, e1_ref, r2_ref, e2_ref, y_ref, acc_scr):
    e = pl.program_id(1)
    te = u_ref.shape[0]
    tm = xn_ref.shape[0]

    @pl.when(e == 0)
    def _():
        acc_scr[...] = jnp.zeros_like(acc_scr)

    sub = PEER_EXPERT_SUB
    n_sub = te // sub
    xn = xn_ref[...]
    zero = jnp.zeros((PEER_N_KEYS, tm), BF16)

    def pre_act(j):
        return lax.dot_general(u_ref[j * sub:(j + 1) * sub, :], xn, (((1,), (1,)), ((), ())),
                               preferred_element_type=F32)

    pending = [pre_act(j) for j in range(min(PEER_LOOKAHEAD, n_sub))]
    for j in range(n_sub):
        zt = pending.pop(0)
        if j + PEER_LOOKAHEAD < n_sub:
            pending.append(pre_act(j + PEER_LOOKAHEAD))
        act = (0.5 * zt * (1.0 + lax.erf(zt * (2.0 ** -0.5)))).astype(BF16)
        gates = []
        for a_loc in range(sub // PEER_N_KEYS):
            a = e * (te // PEER_N_KEYS) + j * (sub // PEER_N_KEYS) + a_loc
            g = zero
            for h in range(PEER_HEADS):
                n1 = n1_ref[h, pl.ds(a, 1), :].astype(BF16)
                e1 = e1_ref[h, pl.ds(a, 1), :].astype(BF16)
                g = g + jnp.where(r2_ref[h] < n1, e2_ref[h], zero) * e1
            gates.append(g)
        w = act * jnp.concatenate(gates, axis=0)
        acc_scr[...] += jnp.dot(vt_ref[:, j * sub:(j + 1) * sub], w, preferred_element_type=F32)

    @pl.when(e == pl.num_programs(1) - 1)
    def _():
        y_ref[...] = x1_ref[...] + acc_scr[...].T


def _peer_dense(xn2, x1, u_bf, vt_bf, n1, e1, r2, e2, tm, te):
    n = xn2.shape[0]
    n_exp = u_bf.shape[0]
    em = pl.BlockSpec((PEER_HEADS, PEER_N_KEYS, tm), lambda i, e: (0, 0, i))
    return pl.pallas_call(
        _peer_dense_kernel,
        grid=(n // tm, n_exp // te),
        in_specs=[pl.BlockSpec((tm, D_MODEL), lambda i, e: (i, 0)), pl.BlockSpec((tm, D_MODEL), lambda i, e: (i, 0)),
                  pl.BlockSpec((te, D_MODEL), lambda i, e: (e, 0)), pl.BlockSpec((D_MODEL, te), lambda i, e: (0, e)),
                  em, em, em, em],
        out_specs=pl.BlockSpec((tm, D_MODEL), lambda i, e: (i, 0)),
        out_shape=jax.ShapeDtypeStruct((n, D_MODEL), F32),
        scratch_shapes=[pltpu.VMEM((D_MODEL, tm), F32)],
        compiler_params=_cparams(("arbitrary", "arbitrary")),
        name="peer_dense",
    )(xn2, x1, u_bf, vt_bf, n1, e1, r2, e2)


def _state_to_pair(s):
    b = s.shape[0]
    s5 = s.reshape(b, N_PAIR, 2, DK_RET, DV_RET)
    z = jnp.zeros_like(s5[:, :, 0])
    top = jnp.concatenate([s5[:, :, 0], z], axis=-1)
    bot = jnp.concatenate([z, s5[:, :, 1]], axis=-1)
    return jnp.concatenate([top, bot], axis=-2)


def _pair_to_state(sp):
    b = sp.shape[0]
    a = sp[:, :, :DK_RET, :DV_RET]
    c = sp[:, :, DK_RET:, DV_RET:]
    return jnp.stack([a, c], axis=2).reshape(b, H_RET, DK_RET, DV_RET)


ROW_TILE = 512
RET_CHUNK = 256
ATTN_TILE = 512
ATTN_HEADS = 2
ATTN_KV_SUB = 256
ATTN_KV_SUBTILES = 8
PEER_ROUTE_TILE = 256
PEER_TOKEN_TILE = 512
PEER_EXPERT_TILE = 2048


def _layer(x, pos, seq, s0_pair, fox_cache, prm):
    batch = x.shape[0]
    n = batch * seq
    x2d = x.reshape(n, D_MODEL)
    tm = min(ROW_TILE, n)
    cosf, sinf = _rotary_tables(pos)
    if fox_cache is not None:
        cosf = jnp.tile(cosf, (batch, 1))
        sinf = jnp.tile(sinf, (batch, 1))
    rq, rk, rv, rgs, sga, sgb = _proj_ret(x2d, prm['g1'], prm['w_ret'], cosf, sinf, tm)
    blocks_per_seq = max(seq // tm, 1)
    fk, fv, logf_pad, qp, kp, vt, cend = _proj_fox(x2d, prm['g1'], prm['w_fox'], prm['bf_pad'], prm['gq_pad'],
                                             prm['gk_pad'], prm['gk_dense'], prm['pmean'], _tri(tm), tm,
                                             blocks_per_seq)
    logf = logf_pad[:, :H_FOX]
    if fox_cache is None:
        ret_o, s_pair = _retention(rq, rk, rv, s0_pair, batch, seq, min(RET_CHUNK, seq))
        assert tm == min(ATTN_TILE, seq)
        fox_o = _fox_attention(qp, kp, vt, cend, prm['gq_pad'], prm['gk_pad'], batch, seq, min(ATTN_TILE, seq),
                               min(ATTN_KV_SUB, seq), ATTN_KV_SUBTILES, ATTN_HEADS)
    else:
        ret_o, s_pair = _retention(rq, rk, rv, s0_pair, batch, seq, seq)
        k_c, v_c, lf_c = fox_cache
        past = k_c.shape[1]
        fox_o = _fox_sample(qp, fk, fv, k_c.reshape(batch, past, FOX_W), v_c.reshape(batch, past, FOX_W),
                            lf_c.transpose(0, 2, 1), logf.reshape(batch, seq, H_FOX).transpose(0, 2, 1), batch, seq)
    x1, xn2 = _merge(x2d, ret_o, rgs, fox_o, sga, sgb, prm['gn'], prm['w_ro'], prm['w_fo_pad'], prm['w_o'],
                     prm['g2'], tm)
    n1, e1, r2, e2 = _peer_route(xn2, prm['w_q'], prm['k1'], prm['k2'], min(PEER_ROUTE_TILE, n))
    y = _peer_dense(xn2, x1, prm['u'], prm['vt'], n1, e1, r2, e2, min(PEER_TOKEN_TILE, n), PEER_EXPERT_TILE)
    return (y.reshape(batch, seq, D_MODEL), _pair_to_state(s_pair), fk.reshape(batch, seq, H_FOX, DH_FOX),
            fv.reshape(batch, seq, H_FOX, DH_FOX), logf.reshape(batch, seq, H_FOX))


def kernel(x_prompt, x_sample, state_ret, cache_fox_k, cache_fox_v, cache_fox_logf, norm1_g, w_in, b_f, fox_q_g,
           fox_k_g, ret_gn_g, w_ret_out, w_fox_out, w_o, norm2_g, peer_w_q, peer_sub_k1, peer_sub_k2, peer_u,
           peer_v):
    depth = w_in.shape[0]
    batch, seq, _ = x_prompt.shape
    dec_batch, dec_seq, _ = x_sample.shape
    past = cache_fox_k.shape[2]
    xp, xs = x_prompt, x_sample
    outs = [[] for _ in range(8)]
    for l in range(depth):
        w_ret, w_fox, bf_pad, gq_pad, gk_pad, gk_dense, pmean = _prep_mixer_params(w_in[l], b_f[l], fox_q_g[l],
                                                                                  fox_k_g[l])
        prm = dict(
            g1=norm1_g[l].reshape(1, D_MODEL), w_ret=w_ret, w_fox=w_fox, bf_pad=bf_pad, gq_pad=gq_pad, gk_pad=gk_pad,
            gk_dense=gk_dense, pmean=pmean, gn=ret_gn_g[l].reshape(1, RET_V), w_ro=w_ret_out[l].astype(BF16),
            w_fo_pad=jnp.pad(w_fox_out[l].reshape(H_FOX, DH_FOX, D_MODEL),
                             ((0, 0), (0, LANES - DH_FOX), (0, 0))).astype(BF16),
            w_o=w_o[l].astype(BF16), g2=norm2_g[l].reshape(1, D_MODEL), w_q=peer_w_q[l].astype(BF16),
            k1=peer_sub_k1[l].astype(BF16), k2=peer_sub_k2[l].astype(BF16), u=peer_u[l].astype(BF16),
            vt=peer_v[l].T.astype(BF16))
        s0_prompt = jnp.zeros((batch, N_PAIR, 2 * DK_RET, 2 * DV_RET), F32)
        xp, s_p, k_p, v_p, lf_p = _layer(xp, jnp.arange(seq), seq, s0_prompt, None, prm)
        xs, s_s, k_s, v_s, lf_s = _layer(xs, past + jnp.arange(dec_seq), dec_seq, _state_to_pair(state_ret[l]),
                                         (cache_fox_k[l], cache_fox_v[l], cache_fox_logf[l]), prm)
        for lst, val in zip(outs, (s_p, k_p, v_p, lf_p, s_s, k_s, v_s, lf_s)):
            lst.append(val)
    stacked = [jnp.stack(lst, axis=0) for lst in outs]
    return (xp, xs, *stacked)
```

```python
import functools
import math

import jax
import jax.numpy as jnp
import numpy as np
from jax import lax
from jax.experimental import pallas as pl
from jax.experimental.pallas import tpu as pltpu

D_MODEL = 1024
H_RET = 8
DK_RET = 64
DV_RET = 128
H_FOX = 8
DH_FOX = 64
ROPE_BASE = 10000.0
PEER_HEADS = 8
PEER_N_KEYS = 128
PEER_HALF = 128
PEER_TOPK = 16
EPS = 1e-6

RET_QK = H_RET * DK_RET
RET_V = H_RET * DV_RET
FOX_W = H_FOX * DH_FOX
N_PAIR = H_RET // 2
LANES = 128
VMEM_LIMIT = 56 * 1024 * 1024

BF16 = jnp.bfloat16
F32 = jnp.float32
LOG2E = math.log2(math.e)


def _cparams(sem):
    return pltpu.CompilerParams(dimension_semantics=sem, vmem_limit_bytes=VMEM_LIMIT)


def _const_spec(shape):
    nd = len(shape)
    return pl.BlockSpec(shape, lambda *_: (0,) * nd, pipeline_mode=pl.Buffered(1))


def _rms_rows(x, g):
    ms = jnp.mean(x * x, axis=-1, keepdims=True)
    return x * lax.rsqrt(ms + EPS) * g


def _split3(c):
    hi = c.astype(BF16).astype(F32)
    r = c - hi
    mid = r.astype(BF16).astype(F32)
    lo = (r - mid).astype(BF16).astype(F32)
    return hi, mid, lo


def _proj_ret_kernel(x_ref, g_ref, w_ref, cos_ref, sin_ref,
                     rq_ref, rk_ref, rv_ref, rgs_ref, sga_ref, sgb_ref):
    xn = _rms_rows(x_ref[...], g_ref[...]).astype(BF16)
    cosf = cos_ref[...]
    sinf = sin_ref[...]
    tm = xn.shape[0]
    lane = lax.broadcasted_iota(jnp.int32, (tm, RET_QK), 1)
    first_half = (lane % DK_RET) < (DK_RET // 2)

    def rotary(h):
        sw = jnp.where(first_half, pltpu.roll(h, RET_QK - DK_RET // 2, 1), pltpu.roll(h, DK_RET // 2, 1))
        parts = []
        for gidx in range(RET_QK // LANES):
            sl = slice(gidx * LANES, (gidx + 1) * LANES)
            parts.append(h[:, sl] * cosf + sw[:, sl] * sinf)
        return jnp.concatenate(parts, axis=-1)

    def proj(lo, width):
        return jnp.dot(xn, w_ref[:, lo:lo + width], preferred_element_type=F32)

    rq_ref[...] = rotary(proj(0, RET_QK)).astype(BF16)
    rk_ref[...] = (rotary(proj(RET_QK, RET_QK)) * (DK_RET ** -0.5)).astype(BF16)
    off = 2 * RET_QK
    rv_ref[...] = proj(off, RET_V).astype(BF16)
    rg = proj(off + RET_V, RET_V)
    rgs_ref[...] = (rg * jax.nn.sigmoid(rg)).astype(BF16)
    sga_ref[...] = jax.nn.sigmoid(proj(off + 2 * RET_V, D_MODEL)).astype(BF16)
    sgb_ref[...] = jax.nn.sigmoid(proj(off + 2 * RET_V + D_MODEL, D_MODEL)).astype(BF16)


def _proj_ret(x2d, g1, w_ret, cosf, sinf, tm):
    n = x2d.shape[0]
    nblk = n // tm
    tab_blocks = cosf.shape[0] // tm
    row = lambda w: pl.BlockSpec((tm, w), lambda i: (i, 0))
    tab = pl.BlockSpec((tm, LANES), lambda i: (i % tab_blocks, 0))
    widths = (RET_QK, RET_QK, RET_V, RET_V, D_MODEL, D_MODEL)
    return pl.pallas_call(
        _proj_ret_kernel,
        grid=(nblk,),
        in_specs=[row(D_MODEL), _const_spec((1, D_MODEL)), _const_spec(w_ret.shape), tab, tab],
        out_specs=[row(w) for w in widths],
        out_shape=[jax.ShapeDtypeStruct((n, w), BF16) for w in widths],
        compiler_params=_cparams(("arbitrary",)),
        name="proj_ret",
    )(x2d, g1, w_ret, cosf, sinf)


def _proj_fox_kernel(blocks_per_seq, x_ref, g_ref, w_ref, bf_ref, gq_ref, gkp_ref, gkd_ref, pmean_ref, tri_ref,
                     fk_ref, fv_ref, logf_ref, qp_ref, kp_ref, vt_ref, cend_ref, carry_ref):
    i = pl.program_id(0)
    xn = _rms_rows(x_ref[...], g_ref[...]).astype(BF16)
    tm = xn.shape[0]

    def proj(lo, width):
        return jnp.dot(xn, w_ref[:, lo:lo + width], preferred_element_type=F32)

    hk = proj(0, FOX_W)
    sq = hk * hk
    sq_hi = sq.astype(BF16)
    sq_lo = (sq - sq_hi.astype(F32)).astype(BF16)
    ms = (jnp.dot(sq_hi, pmean_ref[...], preferred_element_type=F32)
          + jnp.dot(sq_lo, pmean_ref[...], preferred_element_type=F32))
    fk_ref[...] = hk * lax.rsqrt(ms + EPS) * gkd_ref[...]
    fv_ref[...] = proj(FOX_W, FOX_W)

    z = proj(2 * FOX_W, LANES) + bf_ref[...]
    logf = jnp.minimum(z, 0.0) - jnp.log1p(jnp.exp(-jnp.abs(z)))
    logf_ref[...] = logf

    @pl.when(i % blocks_per_seq == 0)
    def _():
        carry_ref[...] = jnp.zeros_like(carry_ref)

    l_hi, l_mid, l_lo = _split3(logf)
    tri = tri_ref[...]
    c = (jnp.dot(tri, l_hi.astype(BF16), preferred_element_type=F32)
         + jnp.dot(tri, l_mid.astype(BF16), preferred_element_type=F32)
         + jnp.dot(tri, l_lo.astype(BF16), preferred_element_type=F32)) + carry_ref[...]
    carry_ref[...] = c[tm - 1:tm, :]
    cend_ref[0] = jnp.broadcast_to(c[tm - 1:tm, :], cend_ref.shape[1:])

    lane = lax.broadcasted_iota(jnp.int32, (tm, LANES), 1)
    is_vec = lane < DH_FOX
    base = 2 * FOX_W + LANES
    hq_all = proj(base, H_FOX * LANES)
    hk_all = proj(base + H_FOX * LANES, H_FOX * LANES)
    hv_all = proj(base + 2 * H_FOX * LANES, H_FOX * LANES)
    for h in range(H_FOX):
        grp = slice(h * LANES, (h + 1) * LANES)
        ch = jnp.broadcast_to(c[:, h:h + 1], (tm, LANES)) * LOG2E
        c_hi, c_mid, c_lo = _split3(ch)
        one = jnp.ones_like(ch)
        zero = jnp.zeros_like(ch)
        aug_q = jnp.where(lane == 64, c_hi, jnp.where(lane == 65, c_mid, jnp.where(
            lane == 66, c_lo, jnp.where(lane < 70, one, zero))))
        aug_k = jnp.where(lane < 67, one, jnp.where(lane == 67, -c_hi, jnp.where(
            lane == 68, -c_mid, jnp.where(lane == 69, -c_lo, zero))))
        hq = hq_all[:, grp]
        hq = hq * lax.rsqrt(jnp.sum(hq * hq, axis=-1, keepdims=True) * (1.0 / DH_FOX) + EPS) * gq_ref[...]
        qp_ref[h] = jnp.where(is_vec, hq * (DH_FOX ** -0.5 * LOG2E), aug_q).astype(BF16)
        hkp = hk_all[:, grp]
        hkp = hkp * lax.rsqrt(jnp.sum(hkp * hkp, axis=-1, keepdims=True) * (1.0 / DH_FOX) + EPS) * gkp_ref[...]
        kp_ref[h] = jnp.where(is_vec, hkp, aug_k).astype(BF16)
        hv = hv_all[:, grp]
        vt_ref[h] = jnp.where(is_vec, hv, one).T[:vt_ref.shape[1]].astype(BF16)


def _proj_fox(x2d, g1, w_fox, bf_pad, gq_pad, gk_pad, gk_dense, pmean, tri, tm, blocks_per_seq):
    n = x2d.shape[0]
    nblk = n // tm
    row = lambda w: pl.BlockSpec((tm, w), lambda i: (i, 0))
    hm = pl.BlockSpec((H_FOX, tm, LANES), lambda i: (0, i, 0))
    return pl.pallas_call(
        functools.partial(_proj_fox_kernel, blocks_per_seq),
        grid=(nblk,),
        in_specs=[row(D_MODEL), _const_spec((1, D_MODEL)), _const_spec(w_fox.shape), _const_spec((1, LANES)),
                  _const_spec((1, LANES)), _const_spec((1, LANES)), _const_spec((1, FOX_W)),
                  _const_spec((FOX_W, FOX_W)), _const_spec((tm, tm))],
        out_specs=[row(FOX_W), row(FOX_W), row(LANES), hm, hm,
                   pl.BlockSpec((H_FOX, V_ROWS, tm), lambda i: (0, 0, i)),
                   pl.BlockSpec((1, 8, LANES), lambda i: (i, 0, 0))],
        out_shape=[jax.ShapeDtypeStruct((n, FOX_W), F32), jax.ShapeDtypeStruct((n, FOX_W), F32),
                   jax.ShapeDtypeStruct((n, LANES), F32), jax.ShapeDtypeStruct((H_FOX, n, LANES), BF16),
                   jax.ShapeDtypeStruct((H_FOX, n, LANES), BF16), jax.ShapeDtypeStruct((H_FOX, V_ROWS, n), BF16),
                   jax.ShapeDtypeStruct((nblk, 8, LANES), F32)],
        scratch_shapes=[pltpu.VMEM((1, LANES), F32)],
        compiler_params=_cparams(("arbitrary",)),
        name="proj_fox",
    )(x2d, g1, w_fox, bf_pad, gq_pad, gk_pad, gk_dense, pmean, tri)


def _rotary_tables(pos):
    half = DK_RET // 2
    inv = ROPE_BASE ** (-jnp.arange(half, dtype=F32) / half)
    ang = pos.astype(F32)[:, None] * inv[None, :]
    cos = jnp.cos(ang)
    sin = jnp.sin(ang)
    cosf = jnp.concatenate([cos, cos, cos, cos], axis=-1)
    sinf = jnp.concatenate([-sin, sin, -sin, sin], axis=-1)
    return cosf, sinf


def _pad_heads(w, n_heads, dh):
    d = w.shape[0]
    w3 = w.reshape(d, n_heads, dh)
    return jnp.pad(w3, ((0, 0), (0, 0), (0, LANES - dh))).reshape(d, n_heads * LANES)


def _prep_mixer_params(w_in, b_f, fox_q_g, fox_k_g):
    sizes = (RET_QK, RET_QK, RET_V, RET_V, FOX_W, FOX_W, FOX_W, H_FOX, D_MODEL, D_MODEL)
    offs = np.concatenate([[0], np.cumsum(sizes)])
    col = lambda j: w_in[:, offs[j]:offs[j + 1]]
    w_ret = jnp.concatenate([col(0), col(1), col(2), col(3), col(8), col(9)], axis=1).astype(BF16)
    w_fz = jnp.pad(col(7), ((0, 0), (0, LANES - H_FOX)))
    w_fox = jnp.concatenate([col(5), col(6), w_fz, _pad_heads(col(4), H_FOX, DH_FOX),
                             _pad_heads(col(5), H_FOX, DH_FOX), _pad_heads(col(6), H_FOX, DH_FOX)],
                            axis=1).astype(BF16)
    bf_pad = jnp.pad(b_f, (0, LANES - H_FOX)).reshape(1, LANES)
    gq_pad = jnp.pad(fox_q_g, (0, LANES - DH_FOX)).reshape(1, LANES)
    gk_pad = jnp.pad(fox_k_g, (0, LANES - DH_FOX)).reshape(1, LANES)
    gk_dense = jnp.tile(fox_k_g, H_FOX).reshape(1, FOX_W)
    grp = np.arange(FOX_W) // DH_FOX
    pmean = jnp.asarray((grp[:, None] == grp[None, :]).astype(np.float32) / DH_FOX, dtype=BF16)
    return w_ret, w_fox, bf_pad, gq_pad, gk_pad, gk_dense, pmean


def _tri(tm):
    r = np.arange(tm)
    return jnp.asarray((r[None, :] <= r[:, None]).astype(np.float32), dtype=BF16)


def _retention_tables(chunk):
    log_gamma = jnp.log1p(-jnp.exp2(-5.0 - jnp.arange(H_RET, dtype=F32)))
    idx = jnp.arange(chunk, dtype=F32)
    diff = idx[:, None] - idx[None, :]
    dmat = jnp.where(diff >= 0, jnp.exp(jnp.maximum(diff, 0.0)[None] * log_gamma[:, None, None]), 0.0)
    lg_qk = jnp.repeat(log_gamma, DK_RET).reshape(N_PAIR, 1, 2 * DK_RET)
    qdec = jnp.exp((idx + 1.0)[None, :, None] * lg_qk)
    kdec = jnp.exp((chunk - 1.0 - idx)[None, :, None] * lg_qk)
    sdec = jnp.broadcast_to(jnp.exp(chunk * lg_qk).reshape(N_PAIR, 2 * DK_RET, 1), (N_PAIR, 2 * DK_RET, 2 * DV_RET))
    row_head = np.arange(2 * DK_RET) // DK_RET
    col_head = np.arange(2 * DV_RET) // DV_RET
    bmask = jnp.asarray((row_head[:, None] == col_head[None, :]).astype(np.float32))
    return dmat, qdec, kdec, sdec, bmask


def _retention_kernel(q_ref, k_ref, v_ref, s0_ref, dmat_ref, qdec_ref, kdec_ref, sdec_ref, bmask_ref,
                      o_ref, sout_ref, s_scr):
    c = pl.program_id(2)

    @pl.when(c == 0)
    def _():
        s_scr[...] = s0_ref[0, 0]

    q = q_ref[...]
    k = k_ref[...]
    v = v_ref[...]
    lane = lax.broadcasted_iota(jnp.int32, q.shape, 1)
    zero = jnp.zeros_like(q)
    nt = (((1,), (1,)), ((), ()))
    halves = []
    for j in range(2):
        qj = jnp.where((lane < DK_RET) == (j == 0), q, zero)
        sc = lax.dot_general(qj, k, nt, preferred_element_type=F32) * dmat_ref[j]
        halves.append(jnp.dot(sc.astype(BF16), v[:, j * DV_RET:(j + 1) * DV_RET], preferred_element_type=F32))
    inner = jnp.concatenate(halves, axis=-1)
    s = s_scr[...]
    qd = (q.astype(F32) * qdec_ref[0]).astype(BF16)
    cross = jnp.dot(qd, s.astype(BF16), preferred_element_type=F32)
    o_ref[...] = (inner + cross).astype(o_ref.dtype)
    kd = (k.astype(F32) * kdec_ref[0]).astype(BF16)
    upd = lax.dot_general(kd, v, (((0,), (0,)), ((), ())), preferred_element_type=F32)
    s_new = sdec_ref[0] * s + bmask_ref[...] * upd
    s_scr[...] = s_new

    @pl.when(c == pl.num_programs(2) - 1)
    def _():
        sout_ref[0, 0] = s_new


def _retention(rq, rk, rv, s0_pair, batch, seq, chunk):
    n = rq.shape[0]
    nchunk = seq // chunk
    dmat, qdec, kdec, sdec, bmask = _retention_tables(chunk)
    qk_spec = pl.BlockSpec((chunk, 2 * DK_RET), lambda b, p, c: (b * nchunk + c, p))
    v_spec = pl.BlockSpec((chunk, 2 * DV_RET), lambda b, p, c: (b * nchunk + c, p))
    s_spec = pl.BlockSpec((1, 1, 2 * DK_RET, 2 * DV_RET), lambda b, p, c: (b, p, 0, 0))
    pair_tab = lambda shape: pl.BlockSpec((1,) + shape, lambda b, p, c: (p, 0, 0))
    return pl.pallas_call(
        _retention_kernel,
        grid=(batch, N_PAIR, nchunk),
        in_specs=[qk_spec, qk_spec, v_spec, s_spec,
                  pl.BlockSpec((2, chunk, chunk), lambda b, p, c: (p, 0, 0)),
                  pair_tab((chunk, 2 * DK_RET)), pair_tab((chunk, 2 * DK_RET)),
                  pair_tab((2 * DK_RET, 2 * DV_RET)),
                  pl.BlockSpec((2 * DK_RET, 2 * DV_RET), lambda b, p, c: (0, 0))],
        out_specs=[v_spec, s_spec],
        out_shape=[jax.ShapeDtypeStruct((n, RET_V), BF16),
                   jax.ShapeDtypeStruct((batch, N_PAIR, 2 * DK_RET, 2 * DV_RET), F32)],
        scratch_shapes=[pltpu.VMEM((2 * DK_RET, 2 * DV_RET), F32)],
        compiler_params=_cparams(("arbitrary", "arbitrary", "arbitrary")),
        name="retention",
    )(rq, rk, rv, s0_pair, dmat, qdec, kdec, sdec, bmask)


NEG_BIG = -1e30
ATTN_LOOKAHEAD = 2


V_ROWS = DH_FOX + 16


def _fox_attn_kernel(n_sub, nq, qi_ref, kj_ref, ks_ref, q_ref, k_ref, vt_ref, o_ref, m_scr, acc_scr):
    s_idx = pl.program_id(2)
    qi = qi_ref[s_idx]
    kj = kj_ref[s_idx]
    kstart = ks_ref[(pl.program_id(0) * pl.num_programs(1) + pl.program_id(1)) * nq + qi]
    n_heads = q_ref.shape[0]
    tq = q_ref.shape[1]
    sub = k_ref.shape[1] // n_sub
    last_kj = (qi * tq) // (n_sub * sub)

    @pl.when(kj == kstart)
    def _():
        m_scr[...] = jnp.full_like(m_scr, NEG_BIG)
        acc_scr[...] = jnp.zeros_like(acc_scr)

    def sweep(masked):
        qs = [q_ref[g] for g in range(n_heads)]
        ms = [m_scr[g] for g in range(n_heads)]
        accs = [acc_scr[g] for g in range(n_heads)]

        def scores(j):
            return [lax.dot_general(k_ref[g, j * sub:(j + 1) * sub, :], qs[g], (((1,), (1,)), ((), ())),
                                    preferred_element_type=F32) for g in range(n_heads)]

        pending = [scores(j) for j in range(min(ATTN_LOOKAHEAD, n_sub))]
        for j in range(n_sub):
            sts = pending.pop(0)
            if j + ATTN_LOOKAHEAD < n_sub:
                pending.append(scores(j + ATTN_LOOKAHEAD))
            for g in range(n_heads):
                st = sts[g]
                if masked:
                    kv_pos = kj * (n_sub * sub) + j * sub + lax.broadcasted_iota(jnp.int32, st.shape, 0)
                    q_pos = qi * tq + lax.broadcasted_iota(jnp.int32, st.shape, 1)
                    st = jnp.where(kv_pos <= q_pos, st, -jnp.inf)
                m_new = jnp.maximum(ms[g], jnp.max(st, axis=0, keepdims=True))
                alpha = jnp.exp2(ms[g] - m_new)
                p = jnp.exp2(st - m_new).astype(BF16)
                accs[g] = alpha * accs[g] + jnp.dot(vt_ref[g, :, j * sub:(j + 1) * sub], p,
                                                    preferred_element_type=F32)
                ms[g] = m_new
        for g in range(n_heads):
            m_scr[g] = ms[g]
            acc_scr[g] = accs[g]

    @pl.when((kj >= kstart) & (kj < last_kj))
    def _():
        sweep(False)

    @pl.when(kj == last_kj)
    def _():
        sweep(True)
        for g in range(n_heads):
            acc = acc_scr[g]
            o = acc / acc[DH_FOX:DH_FOX + 1, :]
            o = jnp.concatenate([o, jnp.zeros((LANES - V_ROWS, tq), F32)], axis=0)
            o_ref[g] = o.T.astype(o_ref.dtype)


UNDERFLOW_LOG2 = 160.0


def _attn_first_block(cend, g_q, g_k, batch, seq, tq, tkv, n_heads):
    nq = seq // tq
    r = tkv // tq
    nkv = seq // tkv
    qk = 1.02 * (jnp.max(jnp.abs(g_q)) * LOG2E) * (DH_FOX ** 0.5 * jnp.max(jnp.abs(g_k)))
    e = cend[:, 0, :H_FOX].reshape(batch, nq, H_FOX)
    c_q = jnp.concatenate([jnp.zeros((batch, 1, H_FOX), F32), e[:, :-1]], axis=1)
    c_k = e[:, r - 1::r]
    gap = (c_q[:, :, None, :] - c_k[:, None, :, :]) * LOG2E
    skippable = (2.0 * qk + gap) < -UNDERFLOW_LOG2
    last_kj = (jnp.arange(nq) * tq) // tkv
    before_diag = jnp.arange(nkv)[None, :] < last_kj[:, None]
    kstart = jnp.sum(jnp.where(skippable & before_diag[None, :, :, None], 1, 0), axis=2)
    kstart = jnp.min(kstart.reshape(batch, nq, H_FOX // n_heads, n_heads), axis=-1)
    return kstart.transpose(2, 0, 1).reshape(-1).astype(jnp.int32)


def _fox_attention(qp, kp, vt, cend, g_q, g_k, batch, seq, tq, sub, n_sub, n_heads):
    n = qp.shape[1]
    nq = seq // tq
    tkv = min(n_sub * sub, seq)
    n_sub = tkv // sub
    nkv = seq // tkv
    steps = [(i, j) for i in range(nq) for j in range((i * tq) // tkv + 1)]
    qi_tab = np.array([s[0] for s in steps], np.int32)
    kj_tab = np.array([s[1] for s in steps], np.int32)
    kstart = _attn_first_block(cend, g_q, g_k, batch, seq, tq, tkv, n_heads)

    def kv_block(h, b, s, qi, kj, ks):
        return b * nkv + jnp.maximum(kj[s], ks[(h * batch + b) * nq + qi[s]])

    q_spec = pl.BlockSpec((n_heads, tq, LANES), lambda h, b, s, qi, kj, ks: (h, b * nq + qi[s], 0))
    k_spec = pl.BlockSpec((n_heads, tkv, LANES), lambda h, b, s, qi, kj, ks: (h, kv_block(h, b, s, qi, kj, ks), 0))
    vt_spec = pl.BlockSpec((n_heads, V_ROWS, tkv), lambda h, b, s, qi, kj, ks: (h, 0, kv_block(h, b, s, qi, kj, ks)))
    return pl.pallas_call(
        functools.partial(_fox_attn_kernel, n_sub, nq),
        grid_spec=pltpu.PrefetchScalarGridSpec(
            num_scalar_prefetch=3,
            grid=(H_FOX // n_heads, batch, len(steps)),
            in_specs=[q_spec, k_spec, vt_spec],
            out_specs=q_spec,
            scratch_shapes=[pltpu.VMEM((n_heads, 1, tq), F32), pltpu.VMEM((n_heads, V_ROWS, tq), F32)],
        ),
        out_shape=jax.ShapeDtypeStruct((H_FOX, n, LANES), BF16),
        compiler_params=_cparams(("arbitrary", "arbitrary", "arbitrary")),
        name="fox_attention",
    )(jnp.asarray(qi_tab), jnp.asarray(kj_tab), kstart, qp, kp, vt)


def _fox_sample_kernel(qp_ref, kn_ref, vn_ref, kc_ref, vc_ref, lfc_ref, lfn_ref, tric_ref, trin_ref, place_ref,
                       o_ref):
    n_new = qp_ref.shape[1]
    past = kc_ref.shape[1]
    rows = H_FOX * n_new
    q_wide = jnp.concatenate([jnp.dot(qp_ref[h], place_ref[h], preferred_element_type=F32)
                              for h in range(H_FOX)], axis=0).astype(BF16)
    k_new = kn_ref[...].astype(BF16)
    v_new = vn_ref[...].astype(BF16)

    nt = (((1,), (1,)), ((), ()))
    s_c = lax.dot_general(q_wide, kc_ref[0].astype(BF16), nt, preferred_element_type=F32)
    s_n = lax.dot_general(q_wide, k_new, nt, preferred_element_type=F32)

    def cumsum_lanes(x, tri):
        hi, mid, lo = _split3(x)
        return (jnp.dot(hi.astype(BF16), tri, preferred_element_type=F32)
                + jnp.dot(mid.astype(BF16), tri, preferred_element_type=F32)
                + jnp.dot(lo.astype(BF16), tri, preferred_element_type=F32))

    c_c = cumsum_lanes(lfc_ref[0], tric_ref[...])
    c_n = (cumsum_lanes(lfn_ref[0], trin_ref[...]) + c_c[:, past - 1:past]) * LOG2E
    c_c = c_c * LOG2E

    def per_row(x):
        return jnp.concatenate([jnp.broadcast_to(x[h:h + 1], (n_new, x.shape[1])) for h in range(H_FOX)], axis=0)

    cn_rows = per_row(c_n)
    colq = lax.broadcasted_iota(jnp.int32, (rows, n_new), 1)
    rowq = lax.broadcasted_iota(jnp.int32, (rows, n_new), 0) % n_new
    c_q = jnp.sum(jnp.where(colq == rowq, cn_rows, 0.0), axis=-1, keepdims=True)

    s_c = s_c + c_q - per_row(c_c)
    s_n = jnp.where(colq <= rowq, s_n + c_q - cn_rows, -jnp.inf)
    m = jnp.maximum(jnp.max(s_c, axis=-1, keepdims=True), jnp.max(s_n, axis=-1, keepdims=True))
    p_c = jnp.exp2(s_c - m)
    p_n = jnp.exp2(s_n - m)
    denom = jnp.sum(p_c, axis=-1, keepdims=True) + jnp.sum(p_n, axis=-1, keepdims=True)
    pv = (jnp.dot(p_c.astype(BF16), vc_ref[0].astype(BF16), preferred_element_type=F32)
          + jnp.dot(p_n.astype(BF16), v_new, preferred_element_type=F32)) / denom
    pv = pv.astype(BF16)
    for h in range(H_FOX):
        o_ref[h] = lax.dot_general(pv[h * n_new:(h + 1) * n_new], place_ref[h], nt,
                                   preferred_element_type=F32).astype(o_ref.dtype)


def _fox_sample(qp, k_new, v_new, k_cache, v_cache, lf_cache_t, lf_new_t, n_streams, n_new):
    past = k_cache.shape[1]
    r = np.arange(past)
    tric = jnp.asarray((r[:, None] <= r[None, :]).astype(np.float32), dtype=BF16)
    rn = np.arange(n_new)
    trin = jnp.asarray((rn[:, None] <= rn[None, :]).astype(np.float32), dtype=BF16)
    place_np = np.zeros((H_FOX, LANES, FOX_W), np.float32)
    for h in range(H_FOX):
        place_np[h, np.arange(DH_FOX), h * DH_FOX + np.arange(DH_FOX)] = 1.0
    place = jnp.asarray(place_np, dtype=BF16)
    hm = pl.BlockSpec((H_FOX, n_new, LANES), lambda s: (0, s, 0))
    cache = pl.BlockSpec((1, past, FOX_W), lambda s: (s, 0, 0))
    return pl.pallas_call(
        _fox_sample_kernel,
        grid=(n_streams,),
        in_specs=[hm, pl.BlockSpec((n_new, FOX_W), lambda s: (s, 0)), pl.BlockSpec((n_new, FOX_W), lambda s: (s, 0)),
                  cache, cache,
                  pl.BlockSpec((1, H_FOX, past), lambda s: (s, 0, 0)),
                  pl.BlockSpec((1, H_FOX, n_new), lambda s: (s, 0, 0)),
                  _const_spec((past, past)), _const_spec((n_new, n_new)), _const_spec((H_FOX, LANES, FOX_W))],
        out_specs=hm,
        out_shape=jax.ShapeDtypeStruct((H_FOX, n_streams * n_new, LANES), BF16),
        compiler_params=_cparams(("arbitrary",)),
        name="fox_sample",
    )(qp, k_new, v_new, k_cache, v_cache, lf_cache_t, lf_new_t, tric, trin, place)


def _merge_kernel(x_ref, ro_ref, rgs_ref, fo_ref, sga_ref, sgb_ref, gn_ref, wro_ref, wfo_ref, wo_ref, g2_ref,
                  x1_ref, xn2_ref):
    ro = ro_ref[...].astype(F32)
    parts = []
    for h in range(H_RET):
        sl = slice(h * DV_RET, (h + 1) * DV_RET)
        parts.append(_rms_rows(ro[:, sl], gn_ref[:, sl]))
    gated = (rgs_ref[...].astype(F32) * jnp.concatenate(parts, axis=-1)).astype(BF16)
    ret_y = jnp.dot(gated, wro_ref[...], preferred_element_type=F32)
    fo = jnp.concatenate([fo_ref[h] for h in range(H_FOX)], axis=-1)
    fox_y = jnp.dot(fo, wfo_ref[...], preferred_element_type=F32)
    mixed = sga_ref[...].astype(F32) * ret_y + sgb_ref[...].astype(F32) * fox_y
    x1 = x_ref[...] + jnp.dot(mixed.astype(BF16), wo_ref[...], preferred_element_type=F32)
    x1_ref[...] = x1
    xn2_ref[...] = _rms_rows(x1, g2_ref[...]).astype(BF16)


def _merge(x2d, ret_o, rgs, fox_o, sga, sgb, gn, w_ro, w_fo_pad, w_o, g2, tm):
    n = x2d.shape[0]
    row = lambda w: pl.BlockSpec((tm, w), lambda i: (i, 0))
    return pl.pallas_call(
        _merge_kernel,
        grid=(n // tm,),
        in_specs=[row(D_MODEL), row(RET_V), row(RET_V), pl.BlockSpec((H_FOX, tm, LANES), lambda i: (0, i, 0)),
                  row(D_MODEL), row(D_MODEL), _const_spec((1, RET_V)), _const_spec(w_ro.shape),
                  _const_spec(w_fo_pad.shape), _const_spec(w_o.shape), _const_spec((1, D_MODEL))],
        out_specs=[row(D_MODEL), row(D_MODEL)],
        out_shape=[jax.ShapeDtypeStruct((n, D_MODEL), F32), jax.ShapeDtypeStruct((n, D_MODEL), BF16)],
        compiler_params=_cparams(("arbitrary",)),
        name="merge",
    )(x2d, ret_o, rgs, fox_o, sga, sgb, gn, w_ro, w_fo_pad, w_o, g2)


_CAND_ROWS = ([(0, j) for j in range(16)] + [(i, j) for i in range(1, 8) for j in range(8)]
              + [(i, 0) for i in range(8, 16)])
_CAND_VALID = np.array([(i + 1) * (j + 1) <= PEER_TOPK for i, j in _CAND_ROWS])
_CAND_FLAT = np.array([i * PEER_TOPK + j for i, j in _CAND_ROWS], np.float32)


def _extract_topk(scores, order, k, break_ties):
    cur = scores
    rank = jnp.full(scores.shape, float(k), F32)
    vals = []
    big = float(2 ** 20)
    for r in range(k):
        m = jnp.max(cur, axis=0, keepdims=True)
        sel = cur == m
        if break_ties:
            first = jnp.min(jnp.where(sel, order, big), axis=0, keepdims=True)
            sel = order == first
        rank = jnp.where(sel, float(r), rank)
        cur = jnp.where(sel, -jnp.inf, cur)
        vals.append(m)
    return vals, rank


def _peer_route_kernel(xn_ref, wq_ref, k1_ref, k2_ref, cflat_ref, cvalid_ref,
                       n1_ref, e1_ref, r2_ref, e2_ref, q_scr):
    tm = xn_ref.shape[0]
    q_scr[...] = jnp.dot(xn_ref[...], wq_ref[...], preferred_element_type=F32).astype(BF16)
    key_iota = lax.broadcasted_iota(jnp.int32, (PEER_N_KEYS, LANES), 0).astype(F32)
    cflat = jnp.broadcast_to(cflat_ref[...], (len(_CAND_ROWS), LANES))
    cvalid = jnp.broadcast_to(cvalid_ref[...], (len(_CAND_ROWS), LANES)) > 0.5
    nt = (((1,), (1,)), ((), ()))

    def head(h, carry):
        col = pl.multiple_of(h * (2 * PEER_HALF), 2 * PEER_HALF)
        s1 = lax.dot_general(k1_ref[...], q_scr[:, pl.ds(col, PEER_HALF)], nt, preferred_element_type=F32)
        s2 = lax.dot_general(k2_ref[...], q_scr[:, pl.ds(col + PEER_HALF, PEER_HALF)], nt,
                             preferred_element_type=F32)

        def route_column(lanes, break_ties):
            s1c = s1[:, lanes]
            s2c = s2[:, lanes]
            v1, rank1 = _extract_topk(s1c, key_iota, PEER_TOPK, break_ties)
            v2, rank2 = _extract_topk(s2c, key_iota, PEER_TOPK, break_ties)
            v2_lo = jnp.concatenate(v2[:8], axis=0)
            v2_all = jnp.concatenate(v2, axis=0)
            cand = jnp.concatenate([v1[0] + v2_all] + [v1[i] + v2_lo for i in range(1, 8)]
                                   + [jnp.concatenate(v1[8:], axis=0) + v2[0]], axis=0)
            cand = jnp.where(cvalid, cand, -jnp.inf)
            _, crank = _extract_topk(cand, cflat, PEER_TOPK, break_ties)
            picked = crank < float(PEER_TOPK)
            top = v1[0] + v2[0]
            z = jnp.sum(jnp.where(picked, jnp.exp(cand - top), 0.0), axis=0, keepdims=True)
            cnt = jnp.where(picked, 1.0, 0.0)
            n_rows = [jnp.sum(cnt[0:16], axis=0, keepdims=True)]
            n_rows += [jnp.sum(cnt[16 + 8 * (i - 1):16 + 8 * i], axis=0, keepdims=True) for i in range(1, 8)]
            n_rows += [cnt[72 + i:73 + i] for i in range(8)]
            n1 = jnp.zeros_like(s1c)
            for i in range(PEER_TOPK):
                n1 = jnp.where(rank1 == float(i), n_rows[i], n1)
            n1_ref[h, :, lanes] = n1
            e1_ref[h, :, lanes] = jnp.exp(s1c - v1[0]) / z
            r2_ref[h, :, lanes] = rank2.astype(BF16)
            e2_ref[h, :, lanes] = jnp.exp(s2c - v2[0]).astype(BF16)
            n_taken = (jnp.sum(jnp.where(rank1 < float(PEER_TOPK), 1.0, 0.0), axis=0, keepdims=True)
                       + jnp.sum(jnp.where(rank2 < float(PEER_TOPK), 1.0, 0.0), axis=0, keepdims=True)
                       + jnp.sum(cnt, axis=0, keepdims=True))
            return jnp.max(n_taken)

        def route(break_ties):
            taken = [route_column(slice(c * LANES, (c + 1) * LANES), break_ties) for c in range(tm // LANES)]
            return functools.reduce(jnp.maximum, taken)

        most_taken = route(False)

        @pl.when(most_taken > 3.0 * PEER_TOPK)
        def _():
            route(True)

        return carry

    lax.fori_loop(0, PEER_HEADS, head, 0)


def _peer_route(xn2, w_q, k1, k2, tm):
    n = xn2.shape[0]
    ncand = len(_CAND_ROWS)
    cflat = jnp.asarray(_CAND_FLAT).reshape(ncand, 1)
    cvalid = jnp.asarray(_CAND_VALID.astype(np.float32)).reshape(ncand, 1)
    em = pl.BlockSpec((PEER_HEADS, PEER_N_KEYS, tm), lambda i: (0, 0, i))
    return pl.pallas_call(
        _peer_route_kernel,
        grid=(n // tm,),
        in_specs=[pl.BlockSpec((tm, D_MODEL), lambda i: (i, 0)), _const_spec(w_q.shape), _const_spec(k1.shape),
                  _const_spec(k2.shape), _const_spec((ncand, 1)), _const_spec((ncand, 1))],
        out_specs=[em, em, em, em],
        out_shape=[jax.ShapeDtypeStruct((PEER_HEADS, PEER_N_KEYS, n), dt) for dt in (F32, F32, BF16, BF16)],
        scratch_shapes=[pltpu.VMEM((tm, PEER_HEADS * 2 * PEER_HALF), BF16)],
        compiler_params=_cparams(("arbitrary",)),
        name="peer_route",
    )(xn2, w_q, k1, k2, cflat, cvalid)


PEER_EXPERT_SUB = 512
PEER_LOOKAHEAD = 2


def _peer_sub_edges(te):
    return list(range(0, te + 1, min(PEER_EXPERT_SUB, te)))


def _peer_dense_kernel(xn_ref, x1_ref, u_ref, vt_ref, n1_ref, e1_ref, r2_ref, e2_ref, y_ref, acc_scr):
    e = pl.program_id(1)
    te = u_ref.shape[0]
    tm = xn_ref.shape[0]

    @pl.when(e == 0)
    def _():
        acc_scr[...] = jnp.zeros_like(acc_scr)

    edges = _peer_sub_edges(te)
    n_sub = len(edges) - 1
    xn = xn_ref[...]
    zero = jnp.zeros((PEER_N_KEYS, tm), BF16)

    def pre_act(j):
        return lax.dot_general(u_ref[edges[j]:edges[j + 1], :], xn, (((1,), (1,)), ((), ())),
                               preferred_element_type=F32)

    pending = [pre_act(j) for j in range(min(PEER_LOOKAHEAD, n_sub))]
    for j in range(n_sub):
        zt = pending.pop(0)
        if j + PEER_LOOKAHEAD < n_sub:
            pending.append(pre_act(j + PEER_LOOKAHEAD))
        act = (0.5 * zt * (1.0 + lax.erf(zt * (2.0 ** -0.5)))).astype(BF16)
        gates = []
        for a in range(edges[j] // PEER_N_KEYS, edges[j + 1] // PEER_N_KEYS):
            g = zero
            for h in range(PEER_HEADS):
                n1 = jnp.broadcast_to(n1_ref[h, a:a + 1, :], (16, tm)).astype(BF16)
                e1 = jnp.broadcast_to(e1_ref[h, a:a + 1, :], (16, tm)).astype(BF16)
                n1 = jnp.concatenate([n1] * (PEER_N_KEYS // 16), axis=0)
                e1 = jnp.concatenate([e1] * (PEER_N_KEYS // 16), axis=0)
                g = g + jnp.where(r2_ref[h] < n1, e2_ref[h], zero) * e1
            gates.append(g)
        w = act * jnp.concatenate(gates, axis=0)
        acc_scr[...] += jnp.dot(vt_ref[:, edges[j]:edges[j + 1]], w, preferred_element_type=F32)

    @pl.when(e == pl.num_programs(1) - 1)
    def _():
        y_ref[...] = x1_ref[...] + acc_scr[...].T


def _peer_dense(xn2, x1, u_bf, vt_bf, n1, e1, r2, e2, tm, te):
    n = xn2.shape[0]
    n_exp = u_bf.shape[0]
    em = pl.BlockSpec((PEER_HEADS, PEER_N_KEYS, tm), lambda i, e: (0, 0, i))
    em_a = pl.BlockSpec((PEER_HEADS, te // PEER_N_KEYS, tm), lambda i, e: (0, e, i))
    return pl.pallas_call(
        _peer_dense_kernel,
        grid=(n // tm, n_exp // te),
        in_specs=[pl.BlockSpec((tm, D_MODEL), lambda i, e: (i, 0)), pl.BlockSpec((tm, D_MODEL), lambda i, e: (i, 0)),
                  pl.BlockSpec((te, D_MODEL), lambda i, e: (e, 0)), pl.BlockSpec((D_MODEL, te), lambda i, e: (0, e)),
                  em_a, em_a, em, em],
        out_specs=pl.BlockSpec((tm, D_MODEL), lambda i, e: (i, 0)),
        out_shape=jax.ShapeDtypeStruct((n, D_MODEL), F32),
        scratch_shapes=[pltpu.VMEM((D_MODEL, tm), F32)],
        compiler_params=_cparams(("arbitrary", "arbitrary")),
        name="peer_dense",
    )(xn2, x1, u_bf, vt_bf, n1, e1, r2, e2)


def _state_to_pair(s):
    b = s.shape[0]
    s5 = s.reshape(b, N_PAIR, 2, DK_RET, DV_RET)
    z = jnp.zeros_like(s5[:, :, 0])
    top = jnp.concatenate([s5[:, :, 0], z], axis=-1)
    bot = jnp.concatenate([z, s5[:, :, 1]], axis=-1)
    return jnp.concatenate([top, bot], axis=-2)


def _pair_to_state(sp):
    b = sp.shape[0]
    a = sp[:, :, :DK_RET, :DV_RET]
    c = sp[:, :, DK_RET:, DV_RET:]
    return jnp.stack([a, c], axis=2).reshape(b, H_RET, DK_RET, DV_RET)


ROW_TILE = 512
RET_CHUNK = 256
ATTN_TILE = 512
ATTN_HEADS = 2
ATTN_KV_SUB = 256
ATTN_KV_SUBTILES = 8
PEER_ROUTE_TILE = 256
PEER_TOKEN_TILE = 512
PEER_EXPERT_TILE = 2048


def _layer(x, pos, seq, s0_pair, fox_cache, prm):
    batch = x.shape[0]
    n = batch * seq
    x2d = x.reshape(n, D_MODEL)
    tm = min(ROW_TILE, n)
    cosf, sinf = _rotary_tables(pos)
    if fox_cache is not None:
        cosf = jnp.tile(cosf, (batch, 1))
        sinf = jnp.tile(sinf, (batch, 1))
    rq, rk, rv, rgs, sga, sgb = _proj_ret(x2d, prm['g1'], prm['w_ret'], cosf, sinf, tm)
    blocks_per_seq = max(seq // tm, 1)
    fk, fv, logf_pad, qp, kp, vt, cend = _proj_fox(x2d, prm['g1'], prm['w_fox'], prm['bf_pad'], prm['gq_pad'],
                                             prm['gk_pad'], prm['gk_dense'], prm['pmean'], _tri(tm), tm,
                                             blocks_per_seq)
    logf = logf_pad[:, :H_FOX]
    if fox_cache is None:
        ret_o, s_pair = _retention(rq, rk, rv, s0_pair, batch, seq, min(RET_CHUNK, seq))
        assert tm == min(ATTN_TILE, seq)
        fox_o = _fox_attention(qp, kp, vt, cend, prm['gq_pad'], prm['gk_pad'], batch, seq, min(ATTN_TILE, seq),
                               min(ATTN_KV_SUB, seq), ATTN_KV_SUBTILES, ATTN_HEADS)
    else:
        ret_o, s_pair = _retention(rq, rk, rv, s0_pair, batch, seq, seq)
        k_c, v_c, lf_c = fox_cache
        past = k_c.shape[1]
        fox_o = _fox_sample(qp, fk, fv, k_c.reshape(batch, past, FOX_W), v_c.reshape(batch, past, FOX_W),
                            lf_c.transpose(0, 2, 1), logf.reshape(batch, seq, H_FOX).transpose(0, 2, 1), batch, seq)
    x1, xn2 = _merge(x2d, ret_o, rgs, fox_o, sga, sgb, prm['gn'], prm['w_ro'], prm['w_fo_pad'], prm['w_o'],
                     prm['g2'], tm)
    n1, e1, r2, e2 = _peer_route(xn2, prm['w_q'], prm['k1'], prm['k2'], min(PEER_ROUTE_TILE, n))
    y = _peer_dense(xn2, x1, prm['u'], prm['vt'], n1, e1, r2, e2, min(PEER_TOKEN_TILE, n), PEER_EXPERT_TILE)
    return (y.reshape(batch, seq, D_MODEL), _pair_to_state(s_pair), fk.reshape(batch, seq, H_FOX, DH_FOX),
            fv.reshape(batch, seq, H_FOX, DH_FOX), logf.reshape(batch, seq, H_FOX))


def kernel(x_prompt, x_sample, state_ret, cache_fox_k, cache_fox_v, cache_fox_logf, norm1_g, w_in, b_f, fox_q_g,
           fox_k_g, ret_gn_g, w_ret_out, w_fox_out, w_o, norm2_g, peer_w_q, peer_sub_k1, peer_sub_k2, peer_u,
           peer_v):
    depth = w_in.shape[0]
    batch, seq, _ = x_prompt.shape
    dec_batch, dec_seq, _ = x_sample.shape
    past = cache_fox_k.shape[2]
    xp, xs = x_prompt, x_sample
    outs = [[] for _ in range(8)]
    for l in range(depth):
        w_ret, w_fox, bf_pad, gq_pad, gk_pad, gk_dense, pmean = _prep_mixer_params(w_in[l], b_f[l], fox_q_g[l],
                                                                                  fox_k_g[l])
        prm = dict(
            g1=norm1_g[l].reshape(1, D_MODEL), w_ret=w_ret, w_fox=w_fox, bf_pad=bf_pad, gq_pad=gq_pad, gk_pad=gk_pad,
            gk_dense=gk_dense, pmean=pmean, gn=ret_gn_g[l].reshape(1, RET_V), w_ro=w_ret_out[l].astype(BF16),
            w_fo_pad=jnp.pad(w_fox_out[l].reshape(H_FOX, DH_FOX, D_MODEL),
                             ((0, 0), (0, LANES - DH_FOX), (0, 0))).astype(BF16).reshape(H_FOX * LANES, D_MODEL),
            w_o=w_o[l].astype(BF16), g2=norm2_g[l].reshape(1, D_MODEL), w_q=peer_w_q[l].astype(BF16),
            k1=peer_sub_k1[l].astype(BF16), k2=peer_sub_k2[l].astype(BF16), u=peer_u[l].astype(BF16),
            vt=peer_v[l].T.astype(BF16))
        s0_prompt = jnp.zeros((batch, N_PAIR, 2 * DK_RET, 2 * DV_RET), F32)
        xp, s_p, k_p, v_p, lf_p = _layer(xp, jnp.arange(seq), seq, s0_prompt, None, prm)
        xs, s_s, k_s, v_s, lf_s = _layer(xs, past + jnp.arange(dec_seq), dec_seq, _state_to_pair(state_ret[l]),
                                         (cache_fox_k[l], cache_fox_v[l], cache_fox_logf[l]), prm)
        for lst, val in zip(outs, (s_p, k_p, v_p, lf_p, s_s, k_s, v_s, lf_s)):
            lst.append(val)
    stacked = [jnp.stack(lst, axis=0) for lst in outs]
    return (xp, xs, *stacked)
```
